```python
import math
import jax, jax.numpy as jnp
from jax import lax
import numpy as np

D_MODEL = 1024
BATCH = 8
SEQ = 2048
DEPTH = 4
DEC_BATCH = 128
DEC_SEQ = 1
PAST_LEN = 2048
PAGE_SIZE = 128

GROUPS = ((128, 1), (512, 4), (2048, 16))
N_GROUPS = 3
HD_A = 64
H_G = D_MODEL // 256
A_QKV = N_GROUPS * 3 * H_G * HD_A
A_OUT = H_G * HD_A
Q_BLOCK = 128
ATTN_SCALE = HD_A ** -0.5
N_BUCKETS = 32
REL_MAX_DIST = 2048
HB = D_MODEL // 128
DK = 128
DV = 128
B_QK = HB * DK
B_V = HB * DV
CONV_W = 4
CONV_DIM = 2 * B_QK + B_V
CHUNK = 64
N_IN = A_QKV + CONV_DIM + B_V + 2 * HB + 2 * D_MODEL
SPLIT_IDX = [A_QKV, A_QKV + CONV_DIM, A_QKV + CONV_DIM + B_V, A_QKV + CONV_DIM + B_V + HB,
             A_QKV + CONV_DIM + B_V + 2 * HB, A_QKV + CONV_DIM + B_V + 2 * HB + D_MODEL]
D_FF = -(-8 * D_MODEL // (3 * 256)) * 256
EPS = 1e-6

kernel_name = "hybrid_dilated_attn_gated_deltanet_decode_step"


def _rms_norm(x, w):
    x32 = x.astype(jnp.float32)
    y = x32 * lax.rsqrt(jnp.mean(x32 * x32, axis=-1, keepdims=True) + EPS)
    return (y * w.astype(jnp.float32)).astype(x.dtype)


def _l2norm(t):
    t = t.astype(jnp.float32)
    return t * lax.rsqrt(jnp.sum(t * t, axis=-1, keepdims=True) + EPS)


def _t5_bucket(dist):
    max_exact = N_BUCKETS // 2
    d = np.maximum(dist, 1).astype(np.float32)
    large = max_exact + (np.log(d / max_exact) / np.log(REL_MAX_DIST / max_exact)
                         * (N_BUCKETS - max_exact)).astype(np.int32)
    large = np.minimum(large, N_BUCKETS - 1)
    return np.where(dist < max_exact, dist, large).astype(np.int32)


def _dilated_attention(q_list, kv_list, rel_bias):
    B, L = q_list[0].shape[:2]
    qb = min(Q_BLOCK, L)
    n_blk = -(-L // qb)
    pad = n_blk * qb - L
    q_pad = [jnp.pad(q, ((0, 0), (0, pad), (0, 0), (0, 0))) for q in q_list]
    biases = []
    for g, (win, dil) in enumerate(GROUPS):
        bucket = _t5_bucket(np.arange(win // dil + 1) * dil)
        biases.append(rel_bias[bucket][:, g * H_G:(g + 1) * H_G].T.astype(jnp.float32))

    def block(b):
        start = b * qb
        i = start + jnp.arange(qb)
        outs, lses = [], []
        for g, (win, dil) in enumerate(GROUPS):
            kv = kv_list[g]
            l_tot = kv.shape[1]
            lb = l_tot - L
            n_keys = win // dil + 1
            idx = lb + i[:, None] - dil * jnp.arange(n_keys)[None, :]
            valid = idx >= 0
            kv_sel = jnp.take(kv, jnp.clip(idx, 0, l_tot - 1), axis=1)
            qg = lax.dynamic_slice_in_dim(q_pad[g], start, qb, axis=1)
            s = (jnp.einsum('bqhd,bqjhd->bhqj', qg, kv_sel[:, :, :, 0]).astype(jnp.float32) * ATTN_SCALE
                 + biases[g][None, :, None, :])
            s = jnp.where(valid[None, None], s, -jnp.inf)
            m = jnp.max(s, axis=-1, keepdims=True)
            p = jnp.exp(s - m)
            den = jnp.sum(p, axis=-1, keepdims=True)
            o = jnp.einsum('bhqj,bqjhd->bqhd', (p / den).astype(kv.dtype), kv_sel[:, :, :, 1])
            outs.append(o.astype(jnp.float32))
            lses.append(jnp.transpose((m + jnp.log(den))[..., 0], (0, 2, 1)))
        wts = jax.nn.softmax(jnp.stack(lses), axis=0)
        o = jnp.einsum('gbqh,gbqhd->bqhd', wts, jnp.stack(outs))
        return o.astype(q_list[0].dtype)

    o = lax.map(block, jnp.arange(n_blk))
    o = jnp.moveaxis(o, 0, 1).reshape(B, n_blk * qb, H_G, HD_A)
    return o[:, :L]


def _causal_conv(x, buf, w):
    L = x.shape[1]
    xc = jnp.concatenate([buf.astype(x.dtype), x], axis=1)
    y = xc[:, 0:L] * w[0]
    for j in range(1, CONV_W):
        y = y + xc[:, j:j + L] * w[j]
    return jax.nn.silu(y), xc[:, -(CONV_W - 1):]


def _gated_delta_rule(q, k, v, g, beta, S0):
    f32 = jnp.float32
    B, L = q.shape[:2]
    C = min(CHUNK, L)
    n = -(-L // C)
    pad = n * C - L

    def prep(t):
        t = t.astype(f32)
        t = jnp.pad(t, [(0, 0), (0, pad)] + [(0, 0)] * (t.ndim - 2))
        t = t.reshape((B, n, C) + t.shape[2:])
        return jnp.moveaxis(t, 3, 1)

    q, k, v, g, beta = prep(q), prep(k), prep(v), prep(g), prep(beta)
    q = q * (DK ** -0.5)
    kb = k * beta[..., None]
    vb = v * beta[..., None]
    gc = jnp.cumsum(g, axis=-1)
    tril = jnp.tril(jnp.ones((C, C), dtype=bool))
    strict = jnp.tril(jnp.ones((C, C), dtype=bool), -1)
    decay = jnp.exp(jnp.where(tril, gc[..., :, None] - gc[..., None, :], -jnp.inf))
    A = jnp.where(strict, jnp.einsum('bhncd,bhnmd->bhncm', kb, k) * decay, 0.0)
    eye = jnp.eye(C, dtype=f32)
    T = lax.linalg.triangular_solve(A + eye, jnp.broadcast_to(eye, A.shape), left_side=True,
                                    lower=True, unit_diagonal=True)
    u = T @ vb
    w = T @ (kb * jnp.exp(gc)[..., None])
    attn = jnp.einsum('bhncd,bhnmd->bhncm', q, k) * decay

    def step(S, xs):
        q_c, k_c, u_c, w_c, attn_c, gc_c = xs
        v_new = u_c - w_c @ S
        o_c = (q_c * jnp.exp(gc_c)[..., None]) @ S + attn_c @ v_new
        g_last = gc_c[..., -1]
        S = (S * jnp.exp(g_last)[..., None, None]
             + jnp.einsum('bhck,bhcv->bhkv', k_c * jnp.exp(g_last[..., None] - gc_c)[..., None], v_new))
        return S, o_c

    xs = tuple(jnp.moveaxis(t, 2, 0) for t in (q, k, u, w, attn, gc))
    S, o = lax.scan(step, S0.astype(f32), xs)
    o = jnp.moveaxis(o, 0, 2).reshape(B, -1, n * C, DV)[:, :, :L]
    return jnp.transpose(o, (0, 2, 1, 3)), S


def _gated_rmsnorm(o, z, w):
    o32 = o.astype(jnp.float32)
    o32 = o32 * lax.rsqrt(jnp.mean(o32 * o32, axis=-1, keepdims=True) + EPS) * w.astype(jnp.float32)
    return o32 * jax.nn.silu(z.astype(jnp.float32))


def _layer(x, kv_bufs, S0, conv_buf, p, l):
    Bsz, L, _ = x.shape
    h = _rms_norm(x, p['norm_pre_mix'][l])
    proj = h @ p['w_in'][l]
    a_qkv, b_qkv, z, b_pre, a_pre, gate_a, gate_b = jnp.split(proj, SPLIT_IDX, axis=-1)
    a_qkv = a_qkv.reshape(Bsz, L, N_GROUPS, 3, H_G, HD_A)
    q_list = [a_qkv[:, :, g, 0] for g in range(N_GROUPS)]
    kv_new = [a_qkv[:, :, g, 1:] for g in range(N_GROUPS)]
    kv_all = [jnp.concatenate([kv_bufs[g].astype(x.dtype), kv_new[g]], axis=1) for g in range(N_GROUPS)]
    o_a = _dilated_attention(q_list, kv_all, p['rel_bias']).reshape(Bsz, L, A_OUT)
    conv_out, conv_new = _causal_conv(b_qkv, conv_buf, p['conv_w'][l])
    qd, kd, vd = jnp.split(conv_out, [B_QK, 2 * B_QK], axis=-1)
    qd = _l2norm(qd.reshape(Bsz, L, HB, DK))
    kd = _l2norm(kd.reshape(Bsz, L, HB, DK))
    vd = vd.reshape(Bsz, L, HB, DV)
    beta = jax.nn.sigmoid(b_pre.astype(jnp.float32))
    g = -jnp.exp(p['a_log'][l].astype(jnp.float32)) * jax.nn.softplus(
        a_pre.astype(jnp.float32) + p['dt_bias'][l].astype(jnp.float32))
    o_b, S = _gated_delta_rule(qd, kd, vd, g, beta, S0)
    o_b = _gated_rmsnorm(o_b, z.reshape(Bsz, L, HB, DV), p['norm_delta'][l]).reshape(Bsz, L, B_V).astype(x.dtype)
    mix = (jax.nn.sigmoid(gate_a) * (o_a @ p['w_branch_a'][l])
           + jax.nn.sigmoid(gate_b) * (o_b @ p['w_branch_b'][l]))
    x = x + _rms_norm(mix @ p['w_out'][l], p['norm_post_mix'][l])
    h = _rms_norm(x, p['norm_pre_ffn'][l])
    gt, up = jnp.split(h @ p['w_ffn_in'][l], 2, axis=-1)
    x = x + _rms_norm((jax.nn.silu(gt) * up) @ p['w_ffn_out'][l], p['norm_post_ffn'][l])
    return x, kv_new, S.astype(x.dtype), conv_new


def _trunk(x, kv_caches, S_cache, conv_cache, p, is_prompt):
    Bsz = x.shape[0]
    kv_out = [[] for _ in range(N_GROUPS)]
    S_out, conv_out = [], []
    for l in range(DEPTH):
        if is_prompt:
            bufs = [jnp.zeros((Bsz, 0, 2, H_G, HD_A), x.dtype) for _ in range(N_GROUPS)]
            S0 = jnp.zeros((Bsz, HB, DK, DV), jnp.float32)
            cb = jnp.zeros((Bsz, CONV_W - 1, CONV_DIM), x.dtype)
        else:
            bufs = [kv_caches[g][l] for g in range(N_GROUPS)]
            S0 = S_cache[l]
            cb = conv_cache[l]
        x, kv_new, S, cb_new = _layer(x, bufs, S0, cb, p, l)
        for g, (win, _) in enumerate(GROUPS):
            rows = kv_new[g]
            kv_out[g].append(rows[:, -min(win, rows.shape[1]):] if is_prompt else rows)
        S_out.append(S)
        conv_out.append(cb_new)
    return (x, jnp.stack(kv_out[0]), jnp.stack(kv_out[1]), jnp.stack(kv_out[2]),
            jnp.stack(S_out), jnp.stack(conv_out))


def setup_inputs(seed: int = 0) -> dict:
    key = jax.random.key(seed)
    ks = jax.random.split(key, 22)
    f32 = jnp.float32
    nrm = lambda k, shape: jax.random.normal(k, shape, f32)
    lens = [min(w, PAST_LEN) for (w, _) in GROUPS]
    dt = jnp.exp(jax.random.uniform(ks[12], (DEPTH, HB), f32, math.log(1e-3), math.log(1e-1)))
    return {
        'x_prompt': nrm(ks[0], (BATCH, SEQ, D_MODEL)),
        'x_sample': nrm(ks[1], (DEC_BATCH, DEC_SEQ, D_MODEL)),
        'cache_kv_w128': nrm(ks[2], (DEPTH, DEC_BATCH, lens[0], 2, H_G, HD_A)),
        'cache_kv_w512': nrm(ks[3], (DEPTH, DEC_BATCH, lens[1], 2, H_G, HD_A)),
        'cache_kv_w2048': nrm(ks[4], (DEPTH, DEC_BATCH, lens[2], 2, H_G, HD_A)),
        'state_delta': 0.1 * nrm(ks[5], (DEPTH, DEC_BATCH, HB, DK, DV)),
        'state_conv': nrm(ks[6], (DEPTH, DEC_BATCH, CONV_W - 1, CONV_DIM)),
        'rel_bias': 0.5 * nrm(ks[7], (N_BUCKETS, N_GROUPS * H_G)),
        'norm_pre_mix': 1.0 + 0.02 * nrm(ks[8], (DEPTH, D_MODEL)),
        'w_in': nrm(ks[9], (DEPTH, D_MODEL, N_IN)) * D_MODEL ** -0.5,
        'conv_w': nrm(ks[10], (DEPTH, CONV_W, CONV_DIM)) * CONV_W ** -0.5,
        'a_log': jnp.log(jax.random.uniform(ks[11], (DEPTH, HB), f32, 1.0, 16.0)),
        'dt_bias': dt + jnp.log(-jnp.expm1(-dt)),
        'norm_delta': 1.0 + 0.02 * nrm(ks[13], (DEPTH, DV)),
        'w_branch_a': nrm(ks[14], (DEPTH, A_OUT, D_MODEL)) * A_OUT ** -0.5,
        'w_branch_b': nrm(ks[15], (DEPTH, B_V, D_MODEL)) * B_V ** -0.5,
        'w_out': nrm(ks[16], (DEPTH, D_MODEL, D_MODEL)) * D_MODEL ** -0.5,
        'norm_post_mix': 1.0 + 0.02 * nrm(ks[17], (DEPTH, D_MODEL)),
        'norm_pre_ffn': 1.0 + 0.02 * nrm(ks[18], (DEPTH, D_MODEL)),
        'w_ffn_in': nrm(ks[19], (DEPTH, D_MODEL, 2 * D_FF)) * D_MODEL ** -0.5,
        'w_ffn_out': nrm(ks[20], (DEPTH, D_FF, D_MODEL)) * D_FF ** -0.5,
        'norm_post_ffn': 1.0 + 0.02 * nrm(ks[21], (DEPTH, D_MODEL)),
    }


def reference(x_prompt, x_sample, cache_kv_w128, cache_kv_w512, cache_kv_w2048, state_delta, state_conv,
              rel_bias, norm_pre_mix, w_in, conv_w, a_log, dt_bias, norm_delta, w_branch_a, w_branch_b,
              w_out, norm_post_mix, norm_pre_ffn, w_ffn_in, w_ffn_out, norm_post_ffn):
    p = {'rel_bias': rel_bias, 'norm_pre_mix': norm_pre_mix, 'w_in': w_in, 'conv_w': conv_w,
         'a_log': a_log, 'dt_bias': dt_bias, 'norm_delta': norm_delta, 'w_branch_a': w_branch_a,
         'w_branch_b': w_branch_b, 'w_out': w_out, 'norm_post_mix': norm_post_mix,
         'norm_pre_ffn': norm_pre_ffn, 'w_ffn_in': w_ffn_in, 'w_ffn_out': w_ffn_out,
         'norm_post_ffn': norm_post_ffn}
    y_prompt, kv128_p, kv512_p, kv2048_p, delta_p, conv_p = _trunk(
        x_prompt, None, None, None, p, True)
    y_sample, kv128_s, kv512_s, kv2048_s, delta_s, conv_s = _trunk(
        x_sample, (cache_kv_w128, cache_kv_w512, cache_kv_w2048), state_delta, state_conv, p, False)
    return (y_prompt, y_sample, kv128_p, kv512_p, kv2048_p, delta_p, conv_p,
            kv128_s, kv512_s, kv2048_s, delta_s, conv_s)
```

```python
import functools

import numpy as np
import jax
import jax.numpy as jnp
from jax import lax
from jax.experimental import pallas as pl
from jax.experimental.pallas import tpu as pltpu

F32 = jnp.float32
BF16 = jnp.bfloat16

D_MODEL = 1024
DEPTH = 4
GROUPS = ((128, 1), (512, 4), (2048, 16))
N_GROUPS = 3
HD_A = 64
H_G = 4
A_OUT = H_G * HD_A
A_GRP = 3 * A_OUT
A_QKV = N_GROUPS * A_GRP
N_KEYS = 129
Q_BLOCK = 128
ATTN_SCALE = HD_A ** -0.5
N_BUCKETS = 32
REL_MAX_DIST = 2048
HB = 8
DK = 128
DV = 128
B_QK = HB * DK
B_V = HB * DV
CONV_W = 4
CONV_DIM = 2 * B_QK + B_V
CHUNK = 64
D_FF = 2816
EPS = 1e-6
NEG = -1e30

COL_B = A_QKV
COL_Z = COL_B + CONV_DIM
COL_BG = COL_Z + B_V
COL_GATE = COL_BG + 2 * HB
N_IN = COL_GATE + 2 * D_MODEL
ZG_DIM = B_V + 2 * D_MODEL
N_MAIN = A_QKV + CONV_DIM + ZG_DIM
PROJ_TN = 768
N_TILES_A = A_QKV // PROJ_TN
N_TILES_B = CONV_DIM // PROJ_TN
N_TILES_ZG = ZG_DIM // PROJ_TN
LANES = 128
FFN_TF = D_FF // 2

VMEM_LIMIT = 52 * 1024 * 1024


def _cparams(sem):
    return pltpu.CompilerParams(dimension_semantics=sem, vmem_limit_bytes=VMEM_LIMIT)


def _sigmoid(x):
    return 1.0 / (1.0 + jnp.exp(-x))


def _silu(x):
    return x * _sigmoid(x)


def _softplus(x):
    return jnp.maximum(x, 0.0) + jnp.log1p(jnp.exp(-jnp.abs(x)))


def _rms(x, w):
    return x * lax.rsqrt(jnp.mean(x * x, axis=-1, keepdims=True) + EPS) * w


def _dot(a, b):
    return jnp.dot(a.astype(BF16), b.astype(BF16), preferred_element_type=F32)


def _dot_nt(a, b):
    return lax.dot_general(a.astype(BF16), b.astype(BF16), (((1,), (1,)), ((), ())),
                           preferred_element_type=F32)


def _dot_f32(a, b):
    return jnp.dot(a, b, preferred_element_type=F32, precision=lax.Precision.HIGHEST)


def _in_proj_kernel(x_ref, nw_ref, w_ref, wbg_ref, oa_ref, ob_ref, ozg_ref, obg_ref, h_ref):
    j = pl.program_id(1)

    @pl.when(j == 0)
    def _():
        hb = _rms(x_ref[...], nw_ref[...]).astype(BF16)
        h_ref[...] = hb
        obg_ref[...] = jnp.dot(hb, wbg_ref[...], preferred_element_type=F32)

    @pl.when(j < N_TILES_A)
    def _():
        oa_ref[...] = jnp.dot(h_ref[...], w_ref[...], preferred_element_type=F32)

    @pl.when((j >= N_TILES_A) & (j < N_TILES_A + N_TILES_B))
    def _():
        ob_ref[...] = jnp.dot(h_ref[...], w_ref[...], preferred_element_type=F32)

    @pl.when(j >= N_TILES_A + N_TILES_B)
    def _():
        ozg_ref[...] = jnp.dot(h_ref[...], w_ref[...], preferred_element_type=F32)


def _in_proj(x2d, norm_w, w_main, w_bg, l, tm):
    T = x2d.shape[0]
    n_tiles = N_TILES_A + N_TILES_B + N_TILES_ZG
    last_a, first_zg = N_TILES_A - 1, N_TILES_A + N_TILES_B
    return pl.pallas_call(
        _in_proj_kernel,
        grid=(T // tm, n_tiles),
        in_specs=[
            pl.BlockSpec((tm, D_MODEL), lambda i, j: (i, 0)),
            pl.BlockSpec((None, 1, D_MODEL), lambda i, j: (l, 0, 0)),
            pl.BlockSpec((None, D_MODEL, PROJ_TN), lambda i, j: (l, 0, j)),
            pl.BlockSpec((None, D_MODEL, LANES), lambda i, j: (l, 0, 0)),
        ],
        out_specs=[
            pl.BlockSpec((tm, PROJ_TN), lambda i, j: (i, jnp.minimum(j, last_a))),
            pl.BlockSpec((tm, PROJ_TN), lambda i, j: (i, jnp.clip(j - N_TILES_A, 0, N_TILES_B - 1))),
            pl.BlockSpec((tm, PROJ_TN), lambda i, j: (i, jnp.maximum(j - first_zg, 0))),
            pl.BlockSpec((tm, LANES), lambda i, j: (i, 0)),
        ],
        out_shape=[
            jax.ShapeDtypeStruct((T, A_QKV), F32),
            jax.ShapeDtypeStruct((T, CONV_DIM), F32),
            jax.ShapeDtypeStruct((T, ZG_DIM), F32),
            jax.ShapeDtypeStruct((T, LANES), F32),
        ],
        scratch_shapes=[pltpu.VMEM((tm, D_MODEL), BF16)],
        compiler_params=_cparams(("parallel", "arbitrary")),
        name="in_proj",
    )(x2d, norm_w, w_main, w_bg)


def _t5_bucket(dist):
    max_exact = N_BUCKETS // 2
    d = np.maximum(dist, 1).astype(np.float32)
    large = max_exact + (np.log(d / max_exact) / np.log(REL_MAX_DIST / max_exact)
                         * (N_BUCKETS - max_exact)).astype(np.int32)
    large = np.minimum(large, N_BUCKETS - 1)
    return np.where(dist < max_exact, dist, large).astype(np.int32)


def _group_bias(rel_bias, g):
    dil = GROUPS[g][1]
    bucket = _t5_bucket(np.arange(N_KEYS) * dil)
    return rel_bias[bucket][:, g * H_G:(g + 1) * H_G].astype(F32)


def _prompt_bias(rel_bias, g):
    qi = np.arange(Q_BLOCK)[:, None]
    kj = np.arange(2 * Q_BLOCK)[None, :]
    jj = qi + Q_BLOCK - kj
    valid = (jj >= 0) & (jj < N_KEYS)
    vals = _group_bias(rel_bias, g)[np.clip(jj, 0, N_KEYS - 1)]
    vals = jnp.where(valid[:, :, None], vals, NEG)
    return jnp.transpose(vals, (2, 0, 1))


def _attn_prompt_kernel(q_ref, kc_ref, kp_ref, vc_ref, vp_ref, bias_ref, o_ref, lse_ref):
    pb = pl.program_id(2)
    q = q_ref[0] * ATTN_SCALE
    k = jnp.concatenate([kp_ref[0], kc_ref[0]], axis=0).astype(BF16)
    v = jnp.concatenate([vp_ref[0], vc_ref[0]], axis=0).astype(BF16)
    lane_head = lax.broadcasted_iota(jnp.int32, (1, A_OUT), 1) // HD_A
    col = lax.broadcasted_iota(jnp.int32, (1, 2 * Q_BLOCK), 1)
    has_prev = (col + pb * (2 * Q_BLOCK)) >= Q_BLOCK
    o_acc = jnp.zeros((Q_BLOCK, A_OUT), F32)
    lse_acc = jnp.zeros((Q_BLOCK, A_OUT), F32)
    for h in range(H_G):
        hm = lane_head == h
        qh = jnp.where(hm, q, 0.0)
        s = _dot_nt(qh, k) + bias_ref[h]
        s = jnp.where(has_prev, s, NEG)
        m = jnp.max(s, axis=-1, keepdims=True)
        p = jnp.exp(s - m)
        den = jnp.sum(p, axis=-1, keepdims=True)
        oh = _dot(p / den, v)
        o_acc = jnp.where(hm, oh, o_acc)
        lse_acc = jnp.where(hm, m + jnp.log(den), lse_acc)
    o_ref[0] = o_acc
    lse_ref[0] = lse_acc


def _attn_prompt(a_qkv, bias, g, B, L):
    dil = GROUPS[g][1]
    lc = L // dil
    nb = lc // Q_BLOCK
    per_tok = A_QKV // A_OUT
    view = a_qkv.reshape(B, lc, dil * A_QKV)

    def in_spec(part, prev):
        def imap(b, r, pb):
            return (b, jnp.maximum(pb - 1, 0) if prev else pb, r * per_tok + g * 3 + part)
        return pl.BlockSpec((1, Q_BLOCK, A_OUT), imap)

    out_spec = pl.BlockSpec((1, Q_BLOCK, A_OUT), lambda b, r, pb: (b, pb, r))
    o, lse = pl.pallas_call(
        _attn_prompt_kernel,
        grid=(B, dil, nb),
        in_specs=[in_spec(0, False), in_spec(1, False), in_spec(1, True), in_spec(2, False),
                  in_spec(2, True),
                  pl.BlockSpec((H_G, Q_BLOCK, 2 * Q_BLOCK), lambda b, r, pb: (0, 0, 0))],
        out_specs=[out_spec, out_spec],
        out_shape=[jax.ShapeDtypeStruct((B, lc, dil * A_OUT), F32)] * 2,
        compiler_params=_cparams(("parallel", "parallel", "arbitrary")),
        name=f"attn_prompt_g{g}",
    )(view, view, view, view, view, bias)
    return o.reshape(B * L, A_OUT), lse.reshape(B * L, A_OUT)


def _inv_unit_lower(a):
    c = a.shape[0]
    row = lax.broadcasted_iota(jnp.int32, (c, c), 0)
    coli = lax.broadcasted_iota(jnp.int32, (c, c), 1)
    eye = (row == coli).astype(F32)
    p = eye - a
    pw = a
    n = 2
    while n < c:
        pw = _dot_f32(pw, pw)
        p = p + _dot_f32(p, pw)
        n *= 2
    return p


def _delta_prompt_kernel(bq_ref, bk_ref, bv_ref, z_ref, bg_ref, cw_ref, hp_ref, nd_ref,
                         ob_ref, sout_ref, s_ref, xc_ref):
    c = pl.program_id(1)
    C = CHUNK

    @pl.when(c == 0)
    def _():
        s_ref[...] = jnp.zeros_like(s_ref)
        xc_ref[0:8, :] = jnp.zeros((8, CONV_DIM), F32)

    xc_ref[8:8 + C, 0:B_QK] = bq_ref[...]
    xc_ref[8:8 + C, B_QK:2 * B_QK] = bk_ref[...]
    xc_ref[8:8 + C, 2 * B_QK:CONV_DIM] = bv_ref[...]

    def conv(lo):
        y = cw_ref[0:1, lo:lo + DK] * xc_ref[5:5 + C, lo:lo + DK]
        for j in range(1, CONV_W):
            y = y + cw_ref[j:j + 1, lo:lo + DK] * xc_ref[5 + j:5 + j + C, lo:lo + DK]
        return _silu(y)

    bg = bg_ref[...]
    hp = hp_ref[...]
    beta_all = _sigmoid(bg)
    g_all = -jnp.exp(hp[0:1, :]) * _softplus(bg + hp[1:2, :])
    row = lax.broadcasted_iota(jnp.int32, (C, C), 0)
    colc = lax.broadcasted_iota(jnp.int32, (C, C), 1)
    tril = row >= colc
    gc = _dot_f32(tril.astype(F32), g_all)
    r2 = lax.broadcasted_iota(jnp.int32, (LANES, LANES), 0)
    c2 = lax.broadcasted_iota(jnp.int32, (LANES, LANES), 1)
    eye_l = (r2 == c2).astype(F32)
    gct = lax.dot_general(eye_l, gc, (((1,), (1,)), ((), ())), preferred_element_type=F32,
                          precision=lax.Precision.HIGHEST)
    nd = nd_ref[...]

    for h in range(HB):
        lo = h * DK
        q = conv(lo)
        k = conv(B_QK + lo)
        v = conv(2 * B_QK + lo)
        q = q * lax.rsqrt(jnp.sum(q * q, axis=-1, keepdims=True) + EPS)
        k = k * lax.rsqrt(jnp.sum(k * k, axis=-1, keepdims=True) + EPS)
        beta = beta_all[:, h:h + 1]
        gcol = gc[:, HB + h:HB + h + 1]
        grow = gct[HB + h:HB + h + 1, :]
        glast = gc[C - 1:C, HB + h:HB + h + 1]
        qs = q * (DK ** -0.5)
        kb = k * beta
        vb = v * beta
        decay = jnp.exp(jnp.where(tril, gcol - grow, NEG))
        kq = _dot_nt(jnp.concatenate([kb, qs], axis=0), k)
        a = jnp.where(row > colc, kq[0:C] * decay, 0.0)
        attn = kq[C:2 * C] * decay
        t = _inv_unit_lower(a)
        uw = _dot(t, jnp.concatenate([vb, kb * jnp.exp(gcol)], axis=1))
        u = uw[:, 0:DV]
        w = uw[:, DV:DV + DK]
        s = s_ref[h]
        ws_qs = _dot(jnp.concatenate([w, qs * jnp.exp(gcol)], axis=0), s)
        v_new = u - ws_qs[0:C]
        kdec = k * jnp.exp(glast - gcol)
        av = _dot(jnp.concatenate([attn, kdec.T], axis=0), v_new)
        o = ws_qs[C:2 * C] + av[0:C]
        s_ref[h] = s * jnp.exp(glast) + av[C:C + DK]
        o = o * lax.rsqrt(jnp.mean(o * o, axis=-1, keepdims=True) + EPS) * nd
        ob_ref[:, lo:lo + DV] = (o * _silu(z_ref[:, lo:lo + DV])).astype(ob_ref.dtype)

    xc_ref[0:8, :] = xc_ref[C:C + 8, :]

    @pl.when(c == pl.num_programs(1) - 1)
    def _():
        sout_ref[0] = s_ref[...]


def _delta_prompt(b_qkv, zg, bg, conv_w, head_params, norm_delta, l, B, L):
    n = L // CHUNK
    row_spec = lambda part: pl.BlockSpec((CHUNK, B_QK), lambda b, c: (b * n + c, part))
    return pl.pallas_call(
        _delta_prompt_kernel,
        grid=(B, n),
        in_specs=[
            row_spec(0), row_spec(1), row_spec(2),
            pl.BlockSpec((CHUNK, B_V), lambda b, c: (b * n + c, 0)),
            pl.BlockSpec((CHUNK, LANES), lambda b, c: (b * n + c, 0)),
            pl.BlockSpec((None, CONV_W, CONV_DIM), lambda b, c: (l, 0, 0)),
            pl.BlockSpec((None, 2, LANES), lambda b, c: (l, 0, 0)),
            pl.BlockSpec((None, 1, DV), lambda b, c: (l, 0, 0)),
        ],
        out_specs=[
            pl.BlockSpec((CHUNK, B_V), lambda b, c: (b * n + c, 0)),
            pl.BlockSpec((1, HB, DK, DV), lambda b, c: (b, 0, 0, 0)),
        ],
        out_shape=[
            jax.ShapeDtypeStruct((B * L, B_V), BF16),
            jax.ShapeDtypeStruct((B, HB, DK, DV), F32),
        ],
        scratch_shapes=[pltpu.VMEM((HB, DK, DV), F32), pltpu.VMEM((8 + CHUNK, CONV_DIM), F32)],
        compiler_params=_cparams(("parallel", "arbitrary")),
        name="delta_prompt",
    )(b_qkv, b_qkv, b_qkv, zg, bg, conv_w, head_params, norm_delta)


def _merge_groups(os_, lses):
    m = jnp.maximum(jnp.maximum(lses[0], lses[1]), lses[2])
    es = [jnp.exp(t - m) for t in lses]
    den = es[0] + es[1] + es[2]
    return (es[0] * os_[0] + es[1] * os_[1] + es[2] * os_[2]) / den


def _mix_tail(o_a, ob_ref, ga_ref, gb_ref, x_ref, wa_ref, wb_ref, wo_ref, nw_ref, out_ref):
    ya = _dot(o_a, wa_ref[...])
    yb = jnp.dot(ob_ref[...], wb_ref[...], preferred_element_type=F32)
    mix = _sigmoid(ga_ref[...]) * ya + _sigmoid(gb_ref[...]) * yb
    y = _dot(mix, wo_ref[...])
    out_ref[...] = x_ref[...] + _rms(y, nw_ref[...])


def _mix_prompt_kernel(o0, l0, o1, l1, o2, l2, ob_ref, ga_ref, gb_ref, x_ref, wa_ref, wb_ref, wo_ref,
                       nw_ref, out_ref):
    o_a = _merge_groups([o0[...], o1[...], o2[...]], [l0[...], l1[...], l2[...]])
    _mix_tail(o_a, ob_ref, ga_ref, gb_ref, x_ref, wa_ref, wb_ref, wo_ref, nw_ref, out_ref)


def _mix_sample_kernel(oa_ref, ob_ref, ga_ref, gb_ref, x_ref, wa_ref, wb_ref, wo_ref, nw_ref, out_ref):
    _mix_tail(oa_ref[...], ob_ref, ga_ref, gb_ref, x_ref, wa_ref, wb_ref, wo_ref, nw_ref, out_ref)


def _mix(attn_parts, o_b, zg, x2d, wa, wb, wo, norm_w, l, tm):
    T = x2d.shape[0]
    rows = lambda w: pl.BlockSpec((tm, w), lambda i: (i, 0))
    weights = [
        pl.BlockSpec((None, A_OUT, D_MODEL), lambda i: (l, 0, 0)),
        pl.BlockSpec((None, B_V, D_MODEL), lambda i: (l, 0, 0)),
        pl.BlockSpec((None, D_MODEL, D_MODEL), lambda i: (l, 0, 0)),
        pl.BlockSpec((None, 1, D_MODEL), lambda i: (l, 0, 0)),
    ]
    common = [
        rows(B_V),
        pl.BlockSpec((tm, D_MODEL), lambda i: (i, 1)),
        pl.BlockSpec((tm, D_MODEL), lambda i: (i, 2)),
        rows(D_MODEL),
    ]
    kern = _mix_prompt_kernel if len(attn_parts) > 1 else _mix_sample_kernel
    return pl.pallas_call(
        kern,
        grid=(T // tm,),
        in_specs=[rows(A_OUT)] * len(attn_parts) + common + weights,
        out_specs=rows(D_MODEL),
        out_shape=jax.ShapeDtypeStruct((T, D_MODEL), F32),
        compiler_params=_cparams(("parallel",)),
        name="mix",
    )(*attn_parts, o_b, zg, zg, x2d, wa, wb, wo, norm_w)


def _ffn_kernel(x_ref, nw1_ref, wg_ref, wu_ref, wd_ref, nw2_ref, out_ref, h_ref, acc_ref):
    j = pl.program_id(1)

    @pl.when(j == 0)
    def _():
        h_ref[...] = _rms(x_ref[...], nw1_ref[...]).astype(BF16)
        acc_ref[...] = jnp.zeros_like(acc_ref)

    h = h_ref[...]
    gt = jnp.dot(h, wg_ref[...], preferred_element_type=F32)
    up = jnp.dot(h, wu_ref[...], preferred_element_type=F32)
    acc_ref[...] += _dot(_silu(gt) * up, wd_ref[...])

    @pl.when(j == pl.num_programs(1) - 1)
    def _():
        out_ref[...] = x_ref[...] + _rms(acc_ref[...], nw2_ref[...])


def _ffn(x2d, nw1, w_in, w_out, nw2, l, tm):
    T = x2d.shape[0]
    nf = D_FF // FFN_TF
    return pl.pallas_call(
        _ffn_kernel,
        grid=(T // tm, nf),
        in_specs=[
            pl.BlockSpec((tm, D_MODEL), lambda i, j: (i, 0)),
            pl.BlockSpec((None, 1, D_MODEL), lambda i, j: (l, 0, 0)),
            pl.BlockSpec((None, D_MODEL, FFN_TF), lambda i, j: (l, 0, j)),
            pl.BlockSpec((None, D_MODEL, FFN_TF), lambda i, j: (l, 0, nf + j)),
            pl.BlockSpec((None, FFN_TF, D_MODEL), lambda i, j: (l, j, 0)),
            pl.BlockSpec((None, 1, D_MODEL), lambda i, j: (l, 0, 0)),
        ],
        out_specs=pl.BlockSpec((tm, D_MODEL), lambda i, j: (i, 0)),
        out_shape=jax.ShapeDtypeStruct((T, D_MODEL), F32),
        scratch_shapes=[pltpu.VMEM((tm, D_MODEL), BF16), pltpu.VMEM((tm, D_MODEL), F32)],
        compiler_params=_cparams(("parallel", "arbitrary")),
        name="ffn",
    )(x2d, nw1, w_in, w_in, w_out, nw2)


N_CACHED = N_KEYS - 1
N_EXT = N_CACHED + 8


def _attn_sample_kernel(a_ref, c0_ref, c1_ref, c2_ref, bias_ref, seg_ref, o_ref):
    bb = a_ref.shape[0]
    seg = seg_ref[...]
    os_, lses = [], []
    for g, c_ref in enumerate((c0_ref, c1_ref, c2_ref)):
        lo = g * A_GRP
        q = a_ref[:, :, lo:lo + A_OUT] * ATTN_SCALE
        kn = a_ref[:, :, lo + A_OUT:lo + 2 * A_OUT]
        vn = a_ref[:, :, lo + 2 * A_OUT:lo + 3 * A_OUT]
        k = jnp.concatenate([c_ref[:, :, 0:A_OUT], jnp.broadcast_to(kn, (bb, 8, A_OUT))], axis=1)
        v = jnp.concatenate([c_ref[:, :, A_OUT:2 * A_OUT], jnp.broadcast_to(vn, (bb, 8, A_OUT))], axis=1)
        prod = (k * q).reshape(bb * N_EXT, A_OUT)
        s = _dot(prod, seg).reshape(bb, N_EXT, A_OUT) + bias_ref[g][None]
        m = jnp.max(s, axis=1, keepdims=True)
        p = jnp.exp(s - m)
        den = jnp.sum(p, axis=1, keepdims=True)
        os_.append(jnp.sum(p * v, axis=1, keepdims=True) / den)
        lses.append(m + jnp.log(den))
    o_ref[...] = _merge_groups(os_, lses)


def _attn_sample(a_qkv, caches, bias, seg, l, bb):
    Bd = a_qkv.shape[0]
    kv_w = 2 * A_OUT
    views, specs = [], []
    for g, (win, dil) in enumerate(GROUPS):
        c = caches[g]
        assert c.shape[2] == N_CACHED * dil, "cache length must equal the group's window"
        views.append(c.reshape(DEPTH, Bd, N_CACHED, dil * kv_w))
        specs.append(pl.BlockSpec((None, bb, N_CACHED, kv_w), lambda i: (l, i, 0, 0)))
    o = pl.pallas_call(
        _attn_sample_kernel,
        grid=(Bd // bb,),
        in_specs=[pl.BlockSpec((bb, 1, A_QKV), lambda i: (i, 0, 0))] + specs + [
            pl.BlockSpec((N_GROUPS, N_EXT, A_OUT), lambda i: (0, 0, 0)),
            pl.BlockSpec((A_OUT, A_OUT), lambda i: (0, 0)),
        ],
        out_specs=pl.BlockSpec((bb, 1, A_OUT), lambda i: (i, 0, 0)),
        out_shape=jax.ShapeDtypeStruct((Bd, 1, A_OUT), F32),
        compiler_params=_cparams(("parallel",)),
        name="attn_sample",
    )(a_qkv.reshape(Bd, 1, A_QKV), *views, bias, seg)
    return o.reshape(Bd, A_OUT)


def _sample_bias(rel_bias):
    out = []
    for g in range(N_GROUPS):
        vals = _group_bias(rel_bias, g)
        by_row = jnp.concatenate([vals[1:][::-1], vals[0:1],
                                  jnp.full((N_EXT - N_KEYS, H_G), NEG, F32)], axis=0)
        out.append(jnp.repeat(by_row, HD_A, axis=1))
    return jnp.stack(out)


def _delta_sample_kernel(bn_ref, cb_ref, z_ref, bg_ref, s_ref, cw_ref, hp_ref, nd_ref, ob_ref, sout_ref):
    bb = bn_ref.shape[0]
    bg = bg_ref[...]
    hp = hp_ref[...]
    beta_all = _sigmoid(bg)
    g_all = -jnp.exp(hp[0:1, :]) * _softplus(bg + hp[1:2, :])
    eg_all = jnp.exp(g_all)
    nd = nd_ref[...]
    r8 = lax.broadcasted_iota(jnp.int32, (1, 8, 1), 1)
    r2 = lax.broadcasted_iota(jnp.int32, (DK, DK), 0)
    c2 = lax.broadcasted_iota(jnp.int32, (DK, DK), 1)
    eye = jnp.broadcast_to((r2 == c2).astype(BF16)[None], (bb, DK, DK))

    def conv(lo):
        y = cw_ref[CONV_W - 1:CONV_W, lo:lo + DK] * bn_ref[:, :, lo:lo + DK]
        for j in range(CONV_W - 1):
            y = y + cw_ref[j:j + 1, lo:lo + DK] * cb_ref[:, j:j + 1, lo:lo + DK]
        return _silu(y)

    for h in range(HB):
        lo = h * DK
        q = conv(lo)
        k = conv(B_QK + lo)
        v = conv(2 * B_QK + lo)
        q = q * lax.rsqrt(jnp.sum(q * q, axis=-1, keepdims=True) + EPS) * (DK ** -0.5)
        k = k * lax.rsqrt(jnp.sum(k * k, axis=-1, keepdims=True) + EPS)
        beta = beta_all[:, :, h:h + 1]
        eg = eg_all[:, :, HB + h:HB + h + 1]
        s = s_ref[:, h]
        w = k * (beta * eg)
        qg = q * eg
        lhs = jnp.where(r8 == 0, w, jnp.where(r8 == 1, qg, 0.0))
        r = jnp.einsum('bmk,bkv->bmv', lhs.astype(BF16), s.astype(BF16), preferred_element_type=F32)
        v_new = v * beta - r[:, 0:1, :]
        attn = jnp.sum(q * k, axis=-1, keepdims=True)
        o = r[:, 1:2, :] + attn * v_new
        kt = jnp.einsum('bij,bmj->bim', eye, jnp.broadcast_to(k, (bb, 8, DK)).astype(BF16),
                        preferred_element_type=F32)
        sout_ref[:, h] = s * eg + kt[:, :, 0:1] * v_new
        o = o * lax.rsqrt(jnp.mean(o * o, axis=-1, keepdims=True) + EPS) * nd
        ob_ref[:, :, lo:lo + DV] = (o * _silu(z_ref[:, :, lo:lo + DV])).astype(ob_ref.dtype)


def _delta_sample(b_qkv, conv_state, zg, bg, state, conv_w, head_params, norm_delta, l, bb):
    Bd = b_qkv.shape[0]
    row = lambda w: pl.BlockSpec((bb, 1, w), lambda i: (i, 0, 0))
    o_b, s_new = pl.pallas_call(
        _delta_sample_kernel,
        grid=(Bd // bb,),
        in_specs=[
            row(CONV_DIM),
            pl.BlockSpec((None, bb, CONV_W - 1, CONV_DIM), lambda i: (l, i, 0, 0)),
            row(B_V),
            row(LANES),
            pl.BlockSpec((None, bb, HB, DK, DV), lambda i: (l, i, 0, 0, 0)),
            pl.BlockSpec((None, CONV_W, CONV_DIM), lambda i: (l, 0, 0)),
            pl.BlockSpec((None, 2, LANES), lambda i: (l, 0, 0)),
            pl.BlockSpec((None, 1, DV), lambda i: (l, 0, 0)),
        ],
        out_specs=[
            row(B_V),
            pl.BlockSpec((bb, HB, DK, DV), lambda i: (i, 0, 0, 0)),
        ],
        out_shape=[
            jax.ShapeDtypeStruct((Bd, 1, B_V), BF16),
            jax.ShapeDtypeStruct((Bd, HB, DK, DV), F32),
        ],
        compiler_params=_cparams(("parallel",)),
        name="delta_sample",
    )(b_qkv.reshape(Bd, 1, CONV_DIM), conv_state, zg.reshape(Bd, 1, ZG_DIM), bg.reshape(Bd, 1, LANES),
      state, conv_w, head_params, norm_delta)
    return o_b.reshape(Bd, B_V), s_new


def _prepare(p):
    w_in = p['w_in']
    w_main = jnp.concatenate([w_in[:, :, :COL_BG], w_in[:, :, COL_GATE:]], axis=-1).astype(BF16)
    w_bg = jnp.pad(w_in[:, :, COL_BG:COL_GATE], ((0, 0), (0, 0), (0, LANES - 2 * HB))).astype(BF16)
    pad = lambda t: jnp.pad(t.astype(F32), ((0, 0), (HB, LANES - 2 * HB)))
    head_params = jnp.stack([pad(p['a_log']), pad(p['dt_bias'])], axis=1)
    vec = lambda t: t.astype(F32)[:, None, :]
    return dict(
        w_main=w_main, w_bg=w_bg, head_params=head_params,
        conv_w=p['conv_w'].astype(F32),
        wa=p['w_branch_a'].astype(BF16), wb=p['w_branch_b'].astype(BF16), wo=p['w_out'].astype(BF16),
        w_ffn_in=p['w_ffn_in'].astype(BF16), w_ffn_out=p['w_ffn_out'].astype(BF16),
        norm_pre_mix=vec(p['norm_pre_mix']), norm_post_mix=vec(p['norm_post_mix']),
        norm_pre_ffn=vec(p['norm_pre_ffn']), norm_post_ffn=vec(p['norm_post_ffn']),
        norm_delta=vec(p['norm_delta']),
    )


def _kv_rows(a_qkv, B, L, g, rows):
    t = a_qkv.reshape(B, L, N_GROUPS, 3, H_G, HD_A)
    return t[:, L - rows:, g, 1:]


def _trunk_prompt(x, w, rel_bias):
    B, L, _ = x.shape
    T = B * L
    tm = min(1024, T)
    x2d = x.reshape(T, D_MODEL)
    biases = [_prompt_bias(rel_bias, g) for g in range(N_GROUPS)]
    kv_out = [[] for _ in range(N_GROUPS)]
    s_out, conv_out = [], []
    for l in range(DEPTH):
        a_qkv, b_qkv, zg, bg = _in_proj(x2d, w['norm_pre_mix'], w['w_main'], w['w_bg'], l, tm)
        parts = []
        for g in range(N_GROUPS):
            parts.extend(_attn_prompt(a_qkv, biases[g], g, B, L))
        o_b, s_fin = _delta_prompt(b_qkv, zg, bg, w['conv_w'], w['head_params'], w['norm_delta'], l, B, L)
        x2d = _mix(parts, o_b, zg, x2d, w['wa'], w['wb'], w['wo'], w['norm_post_mix'], l, min(256, T))
        x2d = _ffn(x2d, w['norm_pre_ffn'], w['w_ffn_in'], w['w_ffn_out'], w['norm_post_ffn'], l, min(512, T))
        for g, (win, _) in enumerate(GROUPS):
            kv_out[g].append(_kv_rows(a_qkv, B, L, g, min(win, L)))
        s_out.append(s_fin)
        conv_out.append(b_qkv.reshape(B, L, CONV_DIM)[:, L - (CONV_W - 1):])
    return (x2d.reshape(B, L, D_MODEL), jnp.stack(kv_out[0]), jnp.stack(kv_out[1]), jnp.stack(kv_out[2]),
            jnp.stack(s_out), jnp.stack(conv_out))


def _trunk_sample(x, caches, state, conv_state, w, rel_bias):
    Bd = x.shape[0]
    x2d = x.reshape(Bd, D_MODEL)
    bias = _sample_bias(rel_bias)
    lane_head = np.arange(A_OUT) // HD_A
    seg = jnp.asarray(lane_head[:, None] == lane_head[None, :], BF16)
    bb = min(8, Bd)
    kv_out = [[] for _ in range(N_GROUPS)]
    s_out, conv_out = [], []
    for l in range(DEPTH):
        a_qkv, b_qkv, zg, bg = _in_proj(x2d, w['norm_pre_mix'], w['w_main'], w['w_bg'], l, Bd)
        o_a = _attn_sample(a_qkv, caches, bias, seg, l, bb)
        o_b, s_new = _delta_sample(b_qkv, conv_state, zg, bg, state, w['conv_w'], w['head_params'],
                                   w['norm_delta'], l, bb)
        x2d = _mix([o_a], o_b, zg, x2d, w['wa'], w['wb'], w['wo'], w['norm_post_mix'], l, Bd)
        x2d = _ffn(x2d, w['norm_pre_ffn'], w['w_ffn_in'], w['w_ffn_out'], w['norm_post_ffn'], l, Bd)
        for g in range(N_GROUPS):
            kv_out[g].append(_kv_rows(a_qkv, Bd, 1, g, 1))
        s_out.append(s_new)
        conv_out.append(jnp.concatenate([conv_state[l][:, 1:], b_qkv[:, None, :]], axis=1))
    return (x2d.reshape(Bd, 1, D_MODEL), jnp.stack(kv_out[0]), jnp.stack(kv_out[1]), jnp.stack(kv_out[2]),
            jnp.stack(s_out), jnp.stack(conv_out))


def kernel(x_prompt, x_sample, cache_kv_w128, cache_kv_w512, cache_kv_w2048, state_delta, state_conv,
           rel_bias, norm_pre_mix, w_in, conv_w, a_log, dt_bias, norm_delta, w_branch_a, w_branch_b,
           w_out, norm_post_mix, norm_pre_ffn, w_ffn_in, w_ffn_out, norm_post_ffn):
    p = dict(w_in=w_in, conv_w=conv_w, a_log=a_log, dt_bias=dt_bias, norm_delta=norm_delta,
             w_branch_a=w_branch_a, w_branch_b=w_branch_b, w_out=w_out, norm_pre_mix=norm_pre_mix,
             norm_post_mix=norm_post_mix, norm_pre_ffn=norm_pre_ffn, w_ffn_in=w_ffn_in,
             w_ffn_out=w_ffn_out, norm_post_ffn=norm_post_ffn)
    w = _prepare(p)
    assert x_sample.shape[1] == 1, "the decode trunk handles one new token per sequence"
    y_p, kv0_p, kv1_p, kv2_p, s_p, conv_p = _trunk_prompt(x_prompt, w, rel_bias)
    y_s, kv0_s, kv1_s, kv2_s, s_s, conv_s = _trunk_sample(
        x_sample, (cache_kv_w128, cache_kv_w512, cache_kv_w2048), state_delta, state_conv, w, rel_bias)
    return (y_p, y_s, kv0_p, kv1_p, kv2_p, s_p, conv_p, kv0_s, kv1_s, kv2_s, s_s, conv_s)
```

```python
import functools

import numpy as np
import jax
import jax.numpy as jnp
from jax import lax
from jax.experimental import pallas as pl
from jax.experimental.pallas import tpu as pltpu

F32 = jnp.float32
BF16 = jnp.bfloat16

D_MODEL = 1024
DEPTH = 4
GROUPS = ((128, 1), (512, 4), (2048, 16))
N_GROUPS = 3
HD_A = 64
H_G = 4
A_OUT = H_G * HD_A
A_GRP = 3 * A_OUT
A_QKV = N_GROUPS * A_GRP
N_KEYS = 129
Q_BLOCK = 128
ATTN_SCALE = HD_A ** -0.5
N_BUCKETS = 32
REL_MAX_DIST = 2048
HB = 8
DK = 128
DV = 128
B_QK = HB * DK
B_V = HB * DV
CONV_W = 4
CONV_DIM = 2 * B_QK + B_V
CHUNK = 64
D_FF = 2816
EPS = 1e-6
NEG = -1e30

COL_B = A_QKV
COL_Z = COL_B + CONV_DIM
COL_BG = COL_Z + B_V
COL_GATE = COL_BG + 2 * HB
N_IN = COL_GATE + 2 * D_MODEL
ZG_DIM = B_V + 2 * D_MODEL
N_MAIN = A_QKV + CONV_DIM + ZG_DIM
PROJ_TN = 768
N_TILES_A = A_QKV // PROJ_TN
N_TILES_B = CONV_DIM // PROJ_TN
N_TILES_ZG = ZG_DIM // PROJ_TN
LANES = 128
FFN_TF = D_FF // 2

VMEM_LIMIT = 52 * 1024 * 1024


def _cparams(sem):
    return pltpu.CompilerParams(dimension_semantics=sem, vmem_limit_bytes=VMEM_LIMIT)


def _sigmoid(x):
    return 1.0 / (1.0 + jnp.exp(-x))


def _silu(x):
    return x * _sigmoid(x)


def _softplus(x):
    return jnp.maximum(x, 0.0) + jnp.log1p(jnp.exp(-jnp.abs(x)))


def _rms(x, w):
    return x * lax.rsqrt(jnp.mean(x * x, axis=-1, keepdims=True) + EPS) * w


def _dot(a, b):
    return jnp.dot(a.astype(BF16), b.astype(BF16), preferred_element_type=F32)


def _dot_nt(a, b):
    return lax.dot_general(a.astype(BF16), b.astype(BF16), (((1,), (1,)), ((), ())),
                           preferred_element_type=F32)


def _dot_f32(a, b):
    return jnp.dot(a, b, preferred_element_type=F32, precision=lax.Precision.HIGHEST)


def _in_proj_kernel(x_ref, nw_ref, w_ref, wbg_ref, oa_ref, ob_ref, ozg_ref, obg_ref, *rest):
    kvt_ref = rest[0] if len(rest) == 2 else None
    h_ref = rest[-1]
    j = pl.program_id(1)

    @pl.when(j == 0)
    def _():
        hb = _rms(x_ref[...], nw_ref[...]).astype(BF16)
        h_ref[...] = hb
        obg_ref[...] = jnp.dot(hb, wbg_ref[...], preferred_element_type=F32)

    @pl.when(j < N_TILES_A)
    def _():
        acc = jnp.dot(h_ref[...], w_ref[...], preferred_element_type=F32)
        if kvt_ref is None:
            oa_ref[...] = acc
        else:
            for c in range(PROJ_TN // LANES):
                oa_ref[c] = acc[:, c * LANES:(c + 1) * LANES]
            kvt_ref[...] = acc[:, A_OUT:].T

    @pl.when((j >= N_TILES_A) & (j < N_TILES_A + N_TILES_B))
    def _():
        ob_ref[...] = jnp.dot(h_ref[...], w_ref[...], preferred_element_type=F32)

    @pl.when(j >= N_TILES_A + N_TILES_B)
    def _():
        ozg_ref[...] = jnp.dot(h_ref[...], w_ref[...], preferred_element_type=F32)


def _in_proj(x2d, norm_w, w_main, w_bg, l, tm, seq_len=None):
    T = x2d.shape[0]
    n_tiles = N_TILES_A + N_TILES_B + N_TILES_ZG
    last_a, first_zg = N_TILES_A - 1, N_TILES_A + N_TILES_B
    extra_specs, extra_shapes = [], []
    a_spec = pl.BlockSpec((tm, PROJ_TN), lambda i, j: (i, jnp.minimum(j, last_a)))
    a_shape = jax.ShapeDtypeStruct((T, A_QKV), F32)
    if seq_len is not None:
        per_seq = seq_len // tm
        slabs = PROJ_TN // LANES
        a_spec = pl.BlockSpec((slabs, tm, LANES), lambda i, j: (jnp.minimum(j, last_a), i, 0))
        a_shape = jax.ShapeDtypeStruct((A_QKV // LANES, T, LANES), F32)
        extra_specs.append(pl.BlockSpec((None, None, 2 * A_OUT, tm),
                                        lambda i, j: (i // per_seq, jnp.minimum(j, last_a), 0, i % per_seq)))
        extra_shapes.append(jax.ShapeDtypeStruct((T // seq_len, N_GROUPS, 2 * A_OUT, seq_len), F32))
    return pl.pallas_call(
        _in_proj_kernel,
        grid=(T // tm, n_tiles),
        in_specs=[
            pl.BlockSpec((tm, D_MODEL), lambda i, j: (i, 0)),
            pl.BlockSpec((None, 1, D_MODEL), lambda i, j: (l, 0, 0)),
            pl.BlockSpec((None, D_MODEL, PROJ_TN), lambda i, j: (l, 0, j)),
            pl.BlockSpec((None, D_MODEL, LANES), lambda i, j: (l, 0, 0)),
        ],
        out_specs=[
            a_spec,
            pl.BlockSpec((tm, PROJ_TN), lambda i, j: (i, jnp.clip(j - N_TILES_A, 0, N_TILES_B - 1))),
            pl.BlockSpec((tm, PROJ_TN), lambda i, j: (i, jnp.maximum(j - first_zg, 0))),
            pl.BlockSpec((tm, LANES), lambda i, j: (i, 0)),
        ] + extra_specs,
        out_shape=[
            a_shape,
            jax.ShapeDtypeStruct((T, CONV_DIM), F32),
            jax.ShapeDtypeStruct((T, ZG_DIM), F32),
            jax.ShapeDtypeStruct((T, LANES), F32),
        ] + extra_shapes,
        scratch_shapes=[pltpu.VMEM((tm, D_MODEL), BF16)],
        compiler_params=_cparams(("parallel", "arbitrary")),
        name="in_proj",
    )(x2d, norm_w, w_main, w_bg)


def _t5_bucket(dist):
    max_exact = N_BUCKETS // 2
    d = np.maximum(dist, 1).astype(np.float32)
    large = max_exact + (np.log(d / max_exact) / np.log(REL_MAX_DIST / max_exact)
                         * (N_BUCKETS - max_exact)).astype(np.int32)
    large = np.minimum(large, N_BUCKETS - 1)
    return np.where(dist < max_exact, dist, large).astype(np.int32)


def _group_bias(rel_bias, g):
    dil = GROUPS[g][1]
    bucket = _t5_bucket(np.arange(N_KEYS) * dil)
    return rel_bias[bucket][:, g * H_G:(g + 1) * H_G].astype(F32)


def _prompt_bias(rel_bias, g):
    period = 3 * Q_BLOCK
    vals = _group_bias(rel_bias, g)
    u = jnp.concatenate([vals[::-1], jnp.full((period - N_KEYS, H_G), NEG, F32)], axis=0).T
    flat = jnp.tile(u, (1, Q_BLOCK))[:, :Q_BLOCK * (period - 1)]
    return flat.reshape(H_G, Q_BLOCK, period - 1)[:, :, :2 * Q_BLOCK]


def _attn_prompt_kernel(a_ref, bias_ref, o_ref, lse_ref, qd_ref, kd_ref, vd_ref, od_ref, ld_ref, *, dil):
    L = a_ref.shape[1]
    lc = L // dil
    nb = lc // Q_BLOCK
    halves = A_OUT // LANES
    for r in range(dil):
        rows = pl.ds(r, lc, stride=dil) if dil > 1 else pl.ds(0, lc)
        dst = pl.ds(r * lc, lc)
        for c in range(halves):
            cols = pl.ds(c * LANES, LANES)
            qd_ref[dst, cols] = (a_ref[c, rows, :] * ATTN_SCALE).astype(BF16)
            kd_ref[dst, cols] = a_ref[halves + c, rows, :].astype(BF16)
            vd_ref[dst, cols] = a_ref[2 * halves + c, rows, :].astype(BF16)
    lane_head = lax.broadcasted_iota(jnp.int32, (1, A_OUT), 1) // HD_A
    hms = [lane_head == h for h in range(H_G)]
    o_dst, l_dst = od_ref, ld_ref
    blocks = [(r, pb) for r in range(dil) for pb in range(nb)]
    for i in range(0, len(blocks), 2):
        items = []
        for r, pb in blocks[i:i + 2]:
            off = r * lc + pb * Q_BLOCK
            k0 = off - Q_BLOCK if pb > 0 else off
            items.append((off, k0, pb))
        work = [(it, h) for it in items for h in range(H_G)]
        s = {}
        for (off, k0, pb), h in work:
            q = qd_ref[pl.ds(off, Q_BLOCK), :]
            qh = jnp.where(hms[h], q, jnp.zeros_like(q))
            k = kd_ref[pl.ds(k0, off + Q_BLOCK - k0), :]
            b = bias_ref[h] if pb > 0 else bias_ref[h, :, Q_BLOCK:2 * Q_BLOCK]
            s[off, h] = _dot_nt(qh, k) + b
        pn, lse = {}, {}
        for (off, k0, pb), h in work:
            m = jnp.max(s[off, h], axis=-1, keepdims=True)
            p = jnp.exp(s[off, h] - m)
            den = jnp.sum(p, axis=-1, keepdims=True)
            pn[off, h] = (p / den).astype(BF16)
            lse[off, h] = m + jnp.log(den)
        oh = {}
        for (off, k0, pb), h in work:
            v = vd_ref[pl.ds(k0, off + Q_BLOCK - k0), :]
            oh[off, h] = jnp.dot(pn[off, h], v, preferred_element_type=F32)
        for off, k0, pb in items:
            o_acc, l_acc = oh[off, 0], lse[off, 0]
            for h in range(1, H_G):
                o_acc = jnp.where(hms[h], oh[off, h], o_acc)
                l_acc = jnp.where(hms[h], lse[off, h], l_acc)
            o_dst[pl.ds(off, Q_BLOCK), :] = o_acc
            l_dst[pl.ds(off, Q_BLOCK), :] = jnp.broadcast_to(l_acc, (Q_BLOCK, A_OUT))
    for r in range(dil):
        rows = pl.ds(r, lc, stride=dil) if dil > 1 else pl.ds(0, lc)
        src = pl.ds(r * lc, lc)
        for c in range(halves):
            cols = pl.ds(c * LANES, LANES)
            o_ref[c, rows, :] = od_ref[src, cols]
            lse_ref[c, rows, :] = ld_ref[src, cols]


def _attn_prompt(a_slabs, bias, g, B, L):
    dil = GROUPS[g][1]
    halves = A_OUT // LANES
    out_spec = pl.BlockSpec((halves, L, LANES), lambda b: (0, b, 0))
    dense = lambda dt: pltpu.VMEM((L, A_OUT), dt)
    return pl.pallas_call(
        functools.partial(_attn_prompt_kernel, dil=dil),
        grid=(B,),
        in_specs=[pl.BlockSpec((3 * halves, L, LANES), lambda b: (g, b, 0)),
                  pl.BlockSpec((H_G, Q_BLOCK, 2 * Q_BLOCK), lambda b: (0, 0, 0))],
        out_specs=[out_spec, out_spec],
        out_shape=[jax.ShapeDtypeStruct((halves, B * L, LANES), F32)] * 2,
        scratch_shapes=[dense(BF16), dense(BF16), dense(BF16), dense(F32), dense(F32)],
        compiler_params=_cparams(("parallel",)),
        name=f"attn_prompt_g{g}",
    )(a_slabs, bias)


def _delta_prompt_kernel(bq_ref, bk_ref, bv_ref, z_ref, bg_ref, cw_ref, hp_ref, nd_ref,
                         ob_ref, sout_ref, s_ref, xc_ref):
    c = pl.program_id(1)
    C = CHUNK

    @pl.when(c == 0)
    def _():
        s_ref[...] = jnp.zeros_like(s_ref)
        xc_ref[0:8, :] = jnp.zeros((8, CONV_DIM), F32)

    xc_ref[8:8 + C, 0:B_QK] = bq_ref[...]
    xc_ref[8:8 + C, B_QK:2 * B_QK] = bk_ref[...]
    xc_ref[8:8 + C, 2 * B_QK:CONV_DIM] = bv_ref[...]

    def conv(lo):
        y = cw_ref[0:1, lo:lo + DK] * xc_ref[5:5 + C, lo:lo + DK]
        for j in range(1, CONV_W):
            y = y + cw_ref[j:j + 1, lo:lo + DK] * xc_ref[5 + j:5 + j + C, lo:lo + DK]
        return _silu(y)

    bg = bg_ref[...]
    hp = hp_ref[...]
    beta_all = _sigmoid(bg)
    g_all = -jnp.exp(hp[0:1, :]) * _softplus(bg + hp[1:2, :])
    row = lax.broadcasted_iota(jnp.int32, (C, C), 0)
    colc = lax.broadcasted_iota(jnp.int32, (C, C), 1)
    tril = row >= colc
    gc = _dot_f32(tril.astype(F32), g_all)
    r2 = lax.broadcasted_iota(jnp.int32, (LANES, LANES), 0)
    c2 = lax.broadcasted_iota(jnp.int32, (LANES, LANES), 1)
    eye_l = (r2 == c2).astype(F32)
    gct = lax.dot_general(eye_l, gc, (((1,), (1,)), ((), ())), preferred_element_type=F32,
                          precision=lax.Precision.HIGHEST)
    nd = nd_ref[...]

    heads = range(HB)
    eye = (row == colc).astype(F32)
    ks, kqs, vbs, kbgs, qgs, kdts, decays, elasts = [], [], [], [], [], [], [], []
    for h in heads:
        lo = h * DK
        q = conv(lo)
        k = conv(B_QK + lo)
        v = conv(2 * B_QK + lo)
        q = q * (lax.rsqrt(jnp.sum(q * q, axis=-1, keepdims=True) + EPS) * (DK ** -0.5))
        k = k * lax.rsqrt(jnp.sum(k * k, axis=-1, keepdims=True) + EPS)
        beta = beta_all[:, h:h + 1]
        gcol = gc[:, HB + h:HB + h + 1]
        grow = gct[HB + h:HB + h + 1, :]
        glast = gc[C - 1:C, HB + h:HB + h + 1]
        egc = jnp.exp(gcol)
        kb = k * beta
        ks.append(k.astype(BF16))
        kqs.append(jnp.concatenate([kb, q], axis=0).astype(BF16))
        vbs.append(v * beta)
        kbgs.append(kb * egc)
        qgs.append(q * egc)
        kdts.append((k * jnp.exp(glast - gcol)).T)
        decays.append(jnp.exp(jnp.where(tril, gcol - grow, NEG)))
        elasts.append(jnp.exp(glast))
    kq = [_dot_nt(kqs[h], ks[h]) for h in heads]
    pw = [jnp.where(row > colc, kq[h][0:C] * decays[h], 0.0) for h in heads]
    attn = [kq[h][C:2 * C] * decays[h] for h in heads]
    p = [eye - pw[h] for h in heads]
    pw = [_dot(pw[h], pw[h]) for h in heads]
    n = 4
    while n < C:
        nxt = [_dot(pw[h], pw[h]) for h in heads]
        p = [p[h] + _dot(p[h], pw[h]) for h in heads]
        pw = nxt
        n *= 2
    p = [p[h] + _dot(p[h], pw[h]) for h in heads]
    uw = [_dot(p[h], jnp.concatenate([vbs[h], kbgs[h]], axis=1)) for h in heads]
    s_old = [s_ref[h] for h in heads]
    ws_qs = [_dot(jnp.concatenate([uw[h][:, DV:DV + DK], qgs[h]], axis=0), s_old[h]) for h in heads]
    v_new = [uw[h][:, 0:DV] - ws_qs[h][0:C] for h in heads]
    av = [_dot(jnp.concatenate([attn[h], kdts[h]], axis=0), v_new[h]) for h in heads]
    for h in heads:
        lo = h * DK
        s_ref[h] = s_old[h] * elasts[h] + av[h][C:C + DK]
        o = ws_qs[h][C:2 * C] + av[h][0:C]
        o = o * lax.rsqrt(jnp.mean(o * o, axis=-1, keepdims=True) + EPS) * nd
        ob_ref[:, lo:lo + DV] = (o * _silu(z_ref[:, lo:lo + DV])).astype(ob_ref.dtype)

    xc_ref[0:8, :] = xc_ref[C:C + 8, :]

    @pl.when(c == pl.num_programs(1) - 1)
    def _():
        sout_ref[0] = s_ref[...]


def _delta_prompt(b_qkv, zg, bg, conv_w, head_params, norm_delta, l, B, L):
    n = L // CHUNK
    row_spec = lambda part: pl.BlockSpec((CHUNK, B_QK), lambda b, c: (b * n + c, part))
    return pl.pallas_call(
        _delta_prompt_kernel,
        grid=(B, n),
        in_specs=[
            row_spec(0), row_spec(1), row_spec(2),
            pl.BlockSpec((CHUNK, B_V), lambda b, c: (b * n + c, 0)),
            pl.BlockSpec((CHUNK, LANES), lambda b, c: (b * n + c, 0)),
            pl.BlockSpec((None, CONV_W, CONV_DIM), lambda b, c: (l, 0, 0)),
            pl.BlockSpec((None, 2, LANES), lambda b, c: (l, 0, 0)),
            pl.BlockSpec((None, 1, DV), lambda b, c: (l, 0, 0)),
        ],
        out_specs=[
            pl.BlockSpec((CHUNK, B_V), lambda b, c: (b * n + c, 0)),
            pl.BlockSpec((1, HB, DK, DV), lambda b, c: (b, 0, 0, 0)),
        ],
        out_shape=[
            jax.ShapeDtypeStruct((B * L, B_V), BF16),
            jax.ShapeDtypeStruct((B, HB, DK, DV), F32),
        ],
        scratch_shapes=[pltpu.VMEM((HB, DK, DV), F32), pltpu.VMEM((8 + CHUNK, CONV_DIM), F32)],
        compiler_params=_cparams(("parallel", "arbitrary")),
        name="delta_prompt",
    )(b_qkv, b_qkv, b_qkv, zg, bg, conv_w, head_params, norm_delta)


def _merge_groups(os_, lses):
    m = jnp.maximum(jnp.maximum(lses[0], lses[1]), lses[2])
    es = [jnp.exp(t - m) for t in lses]
    den = es[0] + es[1] + es[2]
    return (es[0] * os_[0] + es[1] * os_[1] + es[2] * os_[2]) / den


def _mix_tail(o_a, ob_ref, ga_ref, gb_ref, x_ref, wa_ref, wb_ref, wo_ref, nw_ref, out_ref):
    ya = _dot(o_a, wa_ref[...])
    yb = jnp.dot(ob_ref[...], wb_ref[...], preferred_element_type=F32)
    mix = _sigmoid(ga_ref[...]) * ya + _sigmoid(gb_ref[...]) * yb
    y = _dot(mix, wo_ref[...])
    out_ref[...] = x_ref[...] + _rms(y, nw_ref[...])


def _mix_prompt_kernel(o0, l0, o1, l1, o2, l2, ob_ref, ga_ref, gb_ref, x_ref, wa_ref, wb_ref, wo_ref,
                       nw_ref, out_ref):
    wide = lambda r: jnp.concatenate([r[c] for c in range(A_OUT // LANES)], axis=1)
    o_a = _merge_groups([wide(o0), wide(o1), wide(o2)], [wide(l0), wide(l1), wide(l2)])
    _mix_tail(o_a, ob_ref, ga_ref, gb_ref, x_ref, wa_ref, wb_ref, wo_ref, nw_ref, out_ref)


def _mix_sample_kernel(oa_ref, ob_ref, ga_ref, gb_ref, x_ref, wa_ref, wb_ref, wo_ref, nw_ref, out_ref):
    _mix_tail(oa_ref[...], ob_ref, ga_ref, gb_ref, x_ref, wa_ref, wb_ref, wo_ref, nw_ref, out_ref)


def _mix(attn_parts, o_b, zg, x2d, wa, wb, wo, norm_w, l, tm):
    T = x2d.shape[0]
    rows = lambda w: pl.BlockSpec((tm, w), lambda i: (i, 0))
    weights = [
        pl.BlockSpec((None, A_OUT, D_MODEL), lambda i: (l, 0, 0)),
        pl.BlockSpec((None, B_V, D_MODEL), lambda i: (l, 0, 0)),
        pl.BlockSpec((None, D_MODEL, D_MODEL), lambda i: (l, 0, 0)),
        pl.BlockSpec((None, 1, D_MODEL), lambda i: (l, 0, 0)),
    ]
    common = [
        rows(B_V),
        pl.BlockSpec((tm, D_MODEL), lambda i: (i, 1)),
        pl.BlockSpec((tm, D_MODEL), lambda i: (i, 2)),
        rows(D_MODEL),
    ]
    if len(attn_parts) > 1:
        kern = _mix_prompt_kernel
        attn_specs = [pl.BlockSpec((A_OUT // LANES, tm, LANES), lambda i: (0, i, 0))] * len(attn_parts)
    else:
        kern = _mix_sample_kernel
        attn_specs = [rows(A_OUT)]
    return pl.pallas_call(
        kern,
        grid=(T // tm,),
        in_specs=attn_specs + common + weights,
        out_specs=rows(D_MODEL),
        out_shape=jax.ShapeDtypeStruct((T, D_MODEL), F32),
        compiler_params=_cparams(("parallel",)),
        name="mix",
    )(*attn_parts, o_b, zg, zg, x2d, wa, wb, wo, norm_w)


def _ffn_kernel(x_ref, nw1_ref, wg_ref, wu_ref, wd_ref, nw2_ref, out_ref, h_ref, acc_ref):
    j = pl.program_id(1)

    @pl.when(j == 0)
    def _():
        h_ref[...] = _rms(x_ref[...], nw1_ref[...]).astype(BF16)
        acc_ref[...] = jnp.zeros_like(acc_ref)

    h = h_ref[...]
    gt = jnp.dot(h, wg_ref[...], preferred_element_type=F32)
    up = jnp.dot(h, wu_ref[...], preferred_element_type=F32)
    acc_ref[...] += _dot(_silu(gt) * up, wd_ref[...])

    @pl.when(j == pl.num_programs(1) - 1)
    def _():
        out_ref[...] = x_ref[...] + _rms(acc_ref[...], nw2_ref[...])


def _ffn(x2d, nw1, w_in, w_out, nw2, l, tm):
    T = x2d.shape[0]
    nf = D_FF // FFN_TF
    return pl.pallas_call(
        _ffn_kernel,
        grid=(T // tm, nf),
        in_specs=[
            pl.BlockSpec((tm, D_MODEL), lambda i, j: (i, 0)),
            pl.BlockSpec((None, 1, D_MODEL), lambda i, j: (l, 0, 0)),
            pl.BlockSpec((None, D_MODEL, FFN_TF), lambda i, j: (l, 0, j)),
            pl.BlockSpec((None, D_MODEL, FFN_TF), lambda i, j: (l, 0, nf + j)),
            pl.BlockSpec((None, FFN_TF, D_MODEL), lambda i, j: (l, j, 0)),
            pl.BlockSpec((None, 1, D_MODEL), lambda i, j: (l, 0, 0)),
        ],
        out_specs=pl.BlockSpec((tm, D_MODEL), lambda i, j: (i, 0)),
        out_shape=jax.ShapeDtypeStruct((T, D_MODEL), F32),
        scratch_shapes=[pltpu.VMEM((tm, D_MODEL), BF16), pltpu.VMEM((tm, D_MODEL), F32)],
        compiler_params=_cparams(("parallel", "arbitrary")),
        name="ffn",
    )(x2d, nw1, w_in, w_in, w_out, nw2)


N_CACHED = N_KEYS - 1


def _attn_sample_kernel(a_ref, c0_ref, c1_ref, c2_ref, b0_ref, b1_ref, b2_ref, bnew_ref, seg_ref, o_ref):
    caches = (c0_ref, c1_ref, c2_ref)
    biases = (b0_ref, b1_ref, b2_ref)
    seg = seg_ref[...]
    a = a_ref[0]
    gh = [(g, h) for g in range(N_GROUPS) for h in range(H_G)]
    rows8 = lambda t: jnp.broadcast_to(t, (8, t.shape[-1]))
    q, vn, s_new = [], [], []
    for g in range(N_GROUPS):
        lo = g * A_GRP
        q.append(a[:, lo:lo + A_OUT] * ATTN_SCALE)
        vn.append(a[:, lo + 2 * A_OUT:lo + 3 * A_OUT])
        s_new.append(_dot(rows8(q[g] * a[:, lo + A_OUT:lo + 2 * A_OUT]), seg) + bnew_ref[g])
    head = lambda t, h: t[:, h * HD_A:(h + 1) * HD_A]
    s = {(g, h): _dot(rows8(head(q[g], h)), caches[g][0, h]) + biases[g][h:h + 1, :] for g, h in gh}
    pn, p_new, lse = {}, {}, {}
    for g, h in gh:
        sn = s_new[g][:, h * HD_A:h * HD_A + 1]
        m = jnp.maximum(jnp.max(s[g, h], axis=-1, keepdims=True), sn)
        p = jnp.exp(s[g, h] - m)
        e_new = jnp.exp(sn - m)
        den = jnp.sum(p, axis=-1, keepdims=True) + e_new
        pn[g, h] = p / den
        p_new[g, h] = e_new / den
        lse[g, h] = m + jnp.log(den)
    o = {(g, h): _dot_nt(pn[g, h], caches[g][1, h]) + p_new[g, h] * head(vn[g], h) for g, h in gh}
    merged = [_merge_groups([o[g, h] for g in range(N_GROUPS)], [lse[g, h] for g in range(N_GROUPS)])
              for h in range(H_G)]
    o_ref[0] = jnp.concatenate(merged, axis=1)[0:1, :]


def _attn_sample(a_qkv, caches_t, biases, bias_new, seg, l):
    Bd = a_qkv.shape[0]
    cache_specs = [pl.BlockSpec((None, None) + c.shape[2:], lambda i: (l, i, 0, 0, 0, 0)) for c in caches_t]
    bias_specs = [pl.BlockSpec(b.shape, lambda i: (0, 0)) for b in biases]
    o = pl.pallas_call(
        _attn_sample_kernel,
        grid=(Bd,),
        in_specs=[pl.BlockSpec((1, 1, A_QKV), lambda i: (i, 0, 0))] + cache_specs + bias_specs + [
            pl.BlockSpec((N_GROUPS, 1, A_OUT), lambda i: (0, 0, 0)),
            pl.BlockSpec((A_OUT, A_OUT), lambda i: (0, 0)),
        ],
        out_specs=pl.BlockSpec((1, 1, A_OUT), lambda i: (i, 0, 0)),
        out_shape=jax.ShapeDtypeStruct((Bd, 1, A_OUT), F32),
        compiler_params=_cparams(("parallel",)),
        name="attn_sample",
    )(a_qkv.reshape(Bd, 1, A_QKV), *caches_t, *biases, bias_new, seg)
    return o.reshape(Bd, A_OUT)


def _sample_bias(rel_bias):
    rows, new = [], []
    for g, (win, dil) in enumerate(GROUPS):
        vals = _group_bias(rel_bias, g)
        sel = vals[1:][::-1][:, None, :]
        skip = jnp.full((N_CACHED, dil - 1, H_G), NEG, F32)
        rows.append(jnp.concatenate([sel, skip], axis=1).reshape(N_CACHED * dil, H_G).T)
        new.append(jnp.repeat(vals[0:1], HD_A, axis=1))
    return rows, jnp.stack(new)


def _delta_sample_kernel(bn_ref, cb_ref, z_ref, bg_ref, s_ref, cw_ref, hp_ref, nd_ref, ob_ref, sout_ref):
    bb = bn_ref.shape[0]
    bg = bg_ref[...]
    hp = hp_ref[...]
    beta_all = _sigmoid(bg)
    g_all = -jnp.exp(hp[0:1, :]) * _softplus(bg + hp[1:2, :])
    eg_all = jnp.exp(g_all)
    nd = nd_ref[...]
    r8 = lax.broadcasted_iota(jnp.int32, (1, 8, 1), 1)
    r2 = lax.broadcasted_iota(jnp.int32, (DK, DK), 0)
    c2 = lax.broadcasted_iota(jnp.int32, (DK, DK), 1)
    eye = jnp.broadcast_to((r2 == c2).astype(BF16)[None], (bb, DK, DK))

    def conv(lo):
        y = cw_ref[CONV_W - 1:CONV_W, lo:lo + DK] * bn_ref[:, :, lo:lo + DK]
        for j in range(CONV_W - 1):
            y = y + cw_ref[j:j + 1, lo:lo + DK] * cb_ref[:, j:j + 1, lo:lo + DK]
        return _silu(y)

    for h in range(HB):
        lo = h * DK
        q = conv(lo)
        k = conv(B_QK + lo)
        v = conv(2 * B_QK + lo)
        q = q * lax.rsqrt(jnp.sum(q * q, axis=-1, keepdims=True) + EPS) * (DK ** -0.5)
        k = k * lax.rsqrt(jnp.sum(k * k, axis=-1, keepdims=True) + EPS)
        beta = beta_all[:, :, h:h + 1]
        eg = eg_all[:, :, HB + h:HB + h + 1]
        s = s_ref[:, h]
        w = k * (beta * eg)
        qg = q * eg
        lhs = jnp.where(r8 == 0, w, jnp.where(r8 == 1, qg, 0.0))
        r = jnp.einsum('bmk,bkv->bmv', lhs.astype(BF16), s.astype(BF16), preferred_element_type=F32)
        v_new = v * beta - r[:, 0:1, :]
        attn = jnp.sum(q * k, axis=-1, keepdims=True)
        o = r[:, 1:2, :] + attn * v_new
        kt = jnp.einsum('bij,bmj->bim', eye, jnp.broadcast_to(k, (bb, 8, DK)).astype(BF16),
                        preferred_element_type=F32)
        sout_ref[:, h] = s * eg + kt[:, :, 0:1] * v_new
        o = o * lax.rsqrt(jnp.mean(o * o, axis=-1, keepdims=True) + EPS) * nd
        ob_ref[:, :, lo:lo + DV] = (o * _silu(z_ref[:, :, lo:lo + DV])).astype(ob_ref.dtype)


def _delta_sample(b_qkv, conv_state, zg, bg, state, conv_w, head_params, norm_delta, l, bb):
    Bd = b_qkv.shape[0]
    row = lambda w: pl.BlockSpec((bb, 1, w), lambda i: (i, 0, 0))
    o_b, s_new = pl.pallas_call(
        _delta_sample_kernel,
        grid=(Bd // bb,),
        in_specs=[
            row(CONV_DIM),
            pl.BlockSpec((None, bb, CONV_W - 1, CONV_DIM), lambda i: (l, i, 0, 0)),
            row(B_V),
            row(LANES),
            pl.BlockSpec((None, bb, HB, DK, DV), lambda i: (l, i, 0, 0, 0)),
            pl.BlockSpec((None, CONV_W, CONV_DIM), lambda i: (l, 0, 0)),
            pl.BlockSpec((None, 2, LANES), lambda i: (l, 0, 0)),
            pl.BlockSpec((None, 1, DV), lambda i: (l, 0, 0)),
        ],
        out_specs=[
            row(B_V),
            pl.BlockSpec((bb, HB, DK, DV), lambda i: (i, 0, 0, 0)),
        ],
        out_shape=[
            jax.ShapeDtypeStruct((Bd, 1, B_V), BF16),
            jax.ShapeDtypeStruct((Bd, HB, DK, DV), F32),
        ],
        compiler_params=_cparams(("parallel",)),
        name="delta_sample",
    )(b_qkv.reshape(Bd, 1, CONV_DIM), conv_state, zg.reshape(Bd, 1, ZG_DIM), bg.reshape(Bd, 1, LANES),
      state, conv_w, head_params, norm_delta)
    return o_b.reshape(Bd, B_V), s_new


def _prepare(p):
    w_in = p['w_in']
    w_main = jnp.concatenate([w_in[:, :, :COL_BG], w_in[:, :, COL_GATE:]], axis=-1).astype(BF16)
    w_bg = jnp.pad(w_in[:, :, COL_BG:COL_GATE], ((0, 0), (0, 0), (0, LANES - 2 * HB))).astype(BF16)
    pad = lambda t: jnp.pad(t.astype(F32), ((0, 0), (HB, LANES - 2 * HB)))
    head_params = jnp.stack([pad(p['a_log']), pad(p['dt_bias'])], axis=1)
    vec = lambda t: t.astype(F32)[:, None, :]
    return dict(
        w_main=w_main, w_bg=w_bg, head_params=head_params,
        conv_w=p['conv_w'].astype(F32),
        wa=p['w_branch_a'].astype(BF16), wb=p['w_branch_b'].astype(BF16), wo=p['w_out'].astype(BF16),
        w_ffn_in=p['w_ffn_in'].astype(BF16), w_ffn_out=p['w_ffn_out'].astype(BF16),
        norm_pre_mix=vec(p['norm_pre_mix']), norm_post_mix=vec(p['norm_post_mix']),
        norm_pre_ffn=vec(p['norm_pre_ffn']), norm_post_ffn=vec(p['norm_post_ffn']),
        norm_delta=vec(p['norm_delta']),
    )


def _kv_rows(a_qkv, B, L, g, rows):
    t = a_qkv.reshape(B, L, N_GROUPS, 3, H_G, HD_A)
    return t[:, L - rows:, g, 1:]


def _trunk_prompt(x, w, rel_bias):
    B, L, _ = x.shape
    T = B * L
    tm = min(1024, T)
    x2d = x.reshape(T, D_MODEL)
    biases = [_prompt_bias(rel_bias, g) for g in range(N_GROUPS)]
    kv_out = [[] for _ in range(N_GROUPS)]
    s_out, conv_out = [], []
    for l in range(DEPTH):
        a_qkv, b_qkv, zg, bg, kvt = _in_proj(x2d, w['norm_pre_mix'], w['w_main'], w['w_bg'], l, tm, seq_len=L)
        parts = []
        for g in range(N_GROUPS):
            parts.extend(_attn_prompt(a_qkv, biases[g], g, B, L))
        o_b, s_fin = _delta_prompt(b_qkv, zg, bg, w['conv_w'], w['head_params'], w['norm_delta'], l, B, L)
        x2d = _mix(parts, o_b, zg, x2d, w['wa'], w['wb'], w['wo'], w['norm_post_mix'], l, min(256, T))
        x2d = _ffn(x2d, w['norm_pre_ffn'], w['w_ffn_in'], w['w_ffn_out'], w['norm_post_ffn'], l, min(512, T))
        for g, (win, _) in enumerate(GROUPS):
            rows = min(win, L)
            kv_out[g].append(kvt[:, g, :, L - rows:].reshape(B, 2, H_G, HD_A, rows))
        s_out.append(s_fin)
        conv_out.append(b_qkv.reshape(B, L, CONV_DIM)[:, L - (CONV_W - 1):])
    kv = [jnp.transpose(jnp.stack(t), (0, 1, 5, 2, 3, 4)) for t in kv_out]
    return (x2d.reshape(B, L, D_MODEL), kv[0], kv[1], kv[2], jnp.stack(s_out), jnp.stack(conv_out))


def _trunk_sample(x, caches, state, conv_state, w, rel_bias):
    Bd = x.shape[0]
    x2d = x.reshape(Bd, D_MODEL)
    biases, bias_new = _sample_bias(rel_bias)
    lane_head = np.arange(A_OUT) // HD_A
    seg = jnp.asarray(lane_head[:, None] == lane_head[None, :], BF16)
    bb = min(8, Bd)
    for c, (win, dil) in zip(caches, GROUPS):
        assert c.shape[2] == N_CACHED * dil, "cache length must equal the group's window"
    caches_t = [jnp.transpose(c, (0, 1, 3, 4, 5, 2)) for c in caches]
    kv_out = [[] for _ in range(N_GROUPS)]
    s_out, conv_out = [], []
    for l in range(DEPTH):
        a_qkv, b_qkv, zg, bg = _in_proj(x2d, w['norm_pre_mix'], w['w_main'], w['w_bg'], l, Bd)
        o_a = _attn_sample(a_qkv, caches_t, biases, bias_new, seg, l)
        o_b, s_new = _delta_sample(b_qkv, conv_state, zg, bg, state, w['conv_w'], w['head_params'],
                                   w['norm_delta'], l, bb)
        x2d = _mix([o_a], o_b, zg, x2d, w['wa'], w['wb'], w['wo'], w['norm_post_mix'], l, Bd)
        x2d = _ffn(x2d, w['norm_pre_ffn'], w['w_ffn_in'], w['w_ffn_out'], w['norm_post_ffn'], l, Bd)
        for g in range(N_GROUPS):
            kv_out[g].append(_kv_rows(a_qkv, Bd, 1, g, 1))
        s_out.append(s_new)
        conv_out.append(jnp.concatenate([conv_state[l][:, 1:], b_qkv[:, None, :]], axis=1))
    return (x2d.reshape(Bd, 1, D_MODEL), jnp.stack(kv_out[0]), jnp.stack(kv_out[1]), jnp.stack(kv_out[2]),
            jnp.stack(s_out), jnp.stack(conv_out))


def kernel(x_prompt, x_sample, cache_kv_w128, cache_kv_w512, cache_kv_w2048, state_delta, state_conv,
           rel_bias, norm_pre_mix, w_in, conv_w, a_log, dt_bias, norm_delta, w_branch_a, w_branch_b,
           w_out, norm_post_mix, norm_pre_ffn, w_ffn_in, w_ffn_out, norm_post_ffn):
    p = dict(w_in=w_in, conv_w=conv_w, a_log=a_log, dt_bias=dt_bias, norm_delta=norm_delta,
             w_branch_a=w_branch_a, w_branch_b=w_branch_b, w_out=w_out, norm_pre_mix=norm_pre_mix,
             norm_post_mix=norm_post_mix, norm_pre_ffn=norm_pre_ffn, w_ffn_in=w_ffn_in,
             w_ffn_out=w_ffn_out, norm_post_ffn=norm_post_ffn)
    w = _prepare(p)
    assert x_sample.shape[1] == 1, "the decode trunk handles one new token per sequence"
    y_p, kv0_p, kv1_p, kv2_p, s_p, conv_p = _trunk_prompt(x_prompt, w, rel_bias)
    y_s, kv0_s, kv1_s, kv2_s, s_s, conv_s = _trunk_sample(
        x_sample, (cache_kv_w128, cache_kv_w512, cache_kv_w2048), state_delta, state_conv, w, rel_bias)
    return (y_p, y_s, kv0_p, kv1_p, kv2_p, s_p, conv_p, kv0_s, kv1_s, kv2_s, s_s, conv_s)
```

```python
import functools

import numpy as np
import jax
import jax.numpy as jnp
from jax import lax
from jax.experimental import pallas as pl
from jax.experimental.pallas import tpu as pltpu

F32 = jnp.float32
BF16 = jnp.bfloat16

D_MODEL = 1024
DEPTH = 4
GROUPS = ((128, 1), (512, 4), (2048, 16))
N_GROUPS = 3
HD_A = 64
H_G = 4
A_OUT = H_G * HD_A
A_GRP = 3 * A_OUT
A_QKV = N_GROUPS * A_GRP
N_KEYS = 129
Q_BLOCK = 128
ATTN_SCALE = HD_A ** -0.5
N_BUCKETS = 32
REL_MAX_DIST = 2048
HB = 8
DK = 128
DV = 128
B_QK = HB * DK
B_V = HB * DV
CONV_W = 4
CONV_DIM = 2 * B_QK + B_V
CHUNK = 64
D_FF = 2816
EPS = 1e-6
NEG = -1e30

COL_B = A_QKV
COL_Z = COL_B + CONV_DIM
COL_BG = COL_Z + B_V
COL_GATE = COL_BG + 2 * HB
N_IN = COL_GATE + 2 * D_MODEL
ZG_DIM = B_V + 2 * D_MODEL
N_MAIN = A_QKV + CONV_DIM + ZG_DIM
PROJ_TN = 768
N_TILES_A = A_QKV // PROJ_TN
N_TILES_B = CONV_DIM // PROJ_TN
N_TILES_ZG = ZG_DIM // PROJ_TN
LANES = 128
FFN_TF = D_FF // 2

VMEM_LIMIT = 52 * 1024 * 1024


def _cparams(sem):
    return pltpu.CompilerParams(dimension_semantics=sem, vmem_limit_bytes=VMEM_LIMIT)


def _sigmoid(x):
    return 1.0 / (1.0 + jnp.exp(-x))


def _silu(x):
    return x * _sigmoid(x)


def _softplus(x):
    return jnp.maximum(x, 0.0) + jnp.log(1.0 + jnp.exp(-jnp.abs(x)))


def _rms(x, w):
    return x * lax.rsqrt(jnp.mean(x * x, axis=-1, keepdims=True) + EPS) * w


def _dot(a, b):
    return jnp.dot(a.astype(BF16), b.astype(BF16), preferred_element_type=F32)


def _dot_nt(a, b):
    return lax.dot_general(a.astype(BF16), b.astype(BF16), (((1,), (1,)), ((), ())),
                           preferred_element_type=F32)


def _dot_f32(a, b):
    return jnp.dot(a, b, preferred_element_type=F32, precision=lax.Precision.HIGHEST)


def _in_proj_kernel(x_ref, nw_ref, w_ref, wbg_ref, oa_ref, ob_ref, ozg_ref, obg_ref, *rest):
    kvt_ref = rest[0] if len(rest) == 2 else None
    h_ref = rest[-1]
    j = pl.program_id(1)
    tile = (lambda: w_ref[j]) if len(w_ref.shape) == 3 else (lambda: w_ref[...])

    @pl.when(j == 0)
    def _():
        hb = _rms(x_ref[...], nw_ref[...]).astype(BF16)
        h_ref[...] = hb
        obg_ref[...] = jnp.dot(hb, wbg_ref[...], preferred_element_type=F32)

    @pl.when(j < N_TILES_A)
    def _():
        acc = jnp.dot(h_ref[...], tile(), preferred_element_type=F32)
        if kvt_ref is None:
            oa_ref[...] = acc
        else:
            for c in range(PROJ_TN // LANES):
                oa_ref[c] = acc[:, c * LANES:(c + 1) * LANES]
            kvt_ref[...] = acc[:, A_OUT:].T

    @pl.when((j >= N_TILES_A) & (j < N_TILES_A + N_TILES_B))
    def _():
        ob_ref[...] = jnp.dot(h_ref[...], tile(), preferred_element_type=F32)

    @pl.when(j >= N_TILES_A + N_TILES_B)
    def _():
        ozg_ref[...] = jnp.dot(h_ref[...], tile(), preferred_element_type=F32)


def _in_proj(x2d, norm_w, w_main, w_bg, l, tm, seq_len=None):
    T = x2d.shape[0]
    n_tiles = N_TILES_A + N_TILES_B + N_TILES_ZG
    last_a, first_zg = N_TILES_A - 1, N_TILES_A + N_TILES_B
    extra_specs, extra_shapes = [], []
    a_spec = pl.BlockSpec((tm, PROJ_TN), lambda i, j: (i, jnp.minimum(j, last_a)))
    a_shape = jax.ShapeDtypeStruct((T, A_QKV), F32)
    w_spec = pl.BlockSpec((None, None, D_MODEL, PROJ_TN), lambda i, j: (l, j, 0, 0))
    if seq_len is not None:
        w_spec = pl.BlockSpec((None, n_tiles, D_MODEL, PROJ_TN), lambda i, j: (l, 0, 0, 0),
                              pipeline_mode=pl.Buffered(1))
        per_seq = seq_len // tm
        slabs = PROJ_TN // LANES
        a_spec = pl.BlockSpec((slabs, tm, LANES), lambda i, j: (jnp.minimum(j, last_a), i, 0))
        a_shape = jax.ShapeDtypeStruct((A_QKV // LANES, T, LANES), F32)
        extra_specs.append(pl.BlockSpec((None, None, 2 * A_OUT, tm),
                                        lambda i, j: (i // per_seq, jnp.minimum(j, last_a), 0, i % per_seq)))
        extra_shapes.append(jax.ShapeDtypeStruct((T // seq_len, N_GROUPS, 2 * A_OUT, seq_len), F32))
    return pl.pallas_call(
        _in_proj_kernel,
        grid=(T // tm, n_tiles),
        in_specs=[
            pl.BlockSpec((tm, D_MODEL), lambda i, j: (i, 0)),
            pl.BlockSpec((None, 1, D_MODEL), lambda i, j: (l, 0, 0)),
            w_spec,
            pl.BlockSpec((None, D_MODEL, LANES), lambda i, j: (l, 0, 0)),
        ],
        out_specs=[
            a_spec,
            pl.BlockSpec((tm, PROJ_TN), lambda i, j: (i, jnp.clip(j - N_TILES_A, 0, N_TILES_B - 1))),
            pl.BlockSpec((tm, PROJ_TN), lambda i, j: (i, jnp.maximum(j - first_zg, 0))),
            pl.BlockSpec((tm, LANES), lambda i, j: (i, 0)),
        ] + extra_specs,
        out_shape=[
            a_shape,
            jax.ShapeDtypeStruct((T, CONV_DIM), F32),
            jax.ShapeDtypeStruct((T, ZG_DIM), F32),
            jax.ShapeDtypeStruct((T, LANES), F32),
        ] + extra_shapes,
        scratch_shapes=[pltpu.VMEM((tm, D_MODEL), BF16)],
        compiler_params=_cparams(("parallel", "arbitrary")),
        name="in_proj",
    )(x2d, norm_w, w_main, w_bg)


def _t5_bucket(dist):
    max_exact = N_BUCKETS // 2
    d = np.maximum(dist, 1).astype(np.float32)
    large = max_exact + (np.log(d / max_exact) / np.log(REL_MAX_DIST / max_exact)
                         * (N_BUCKETS - max_exact)).astype(np.int32)
    large = np.minimum(large, N_BUCKETS - 1)
    return np.where(dist < max_exact, dist, large).astype(np.int32)


def _group_bias(rel_bias, g):
    dil = GROUPS[g][1]
    bucket = _t5_bucket(np.arange(N_KEYS) * dil)
    return rel_bias[bucket][:, g * H_G:(g + 1) * H_G].astype(F32)


def _prompt_bias(rel_bias, g):
    period = 3 * Q_BLOCK
    vals = _group_bias(rel_bias, g)
    u = jnp.concatenate([vals[::-1], jnp.full((period - N_KEYS, H_G), NEG, F32)], axis=0).T
    flat = jnp.tile(u, (1, Q_BLOCK))[:, :Q_BLOCK * (period - 1)]
    return flat.reshape(H_G, Q_BLOCK, period - 1)[:, :, :2 * Q_BLOCK]


def _attn_prompt_kernel(a_ref, bias_ref, o_ref, lse_ref, qd_ref, kd_ref, vd_ref, od_ref, ld_ref, *, dil):
    L = a_ref.shape[1]
    lc = L // dil
    nb = lc // Q_BLOCK
    halves = A_OUT // LANES
    for r in range(dil):
        rows = pl.ds(r, lc, stride=dil) if dil > 1 else pl.ds(0, lc)
        dst = pl.ds(r * lc, lc)
        for c in range(halves):
            cols = pl.ds(c * LANES, LANES)
            qd_ref[dst, cols] = (a_ref[c, rows, :] * ATTN_SCALE).astype(BF16)
            kd_ref[dst, cols] = a_ref[halves + c, rows, :].astype(BF16)
            vd_ref[dst, cols] = a_ref[2 * halves + c, rows, :].astype(BF16)
    lane_head = lax.broadcasted_iota(jnp.int32, (1, A_OUT), 1) // HD_A
    hms = [lane_head == h for h in range(H_G)]
    o_dst, l_dst = od_ref, ld_ref
    blocks = [(r, pb) for r in range(dil) for pb in range(nb)]
    for i in range(0, len(blocks), 2):
        items = []
        for r, pb in blocks[i:i + 2]:
            off = r * lc + pb * Q_BLOCK
            k0 = off - Q_BLOCK if pb > 0 else off
            items.append((off, k0, pb))
        work = [(it, h) for it in items for h in range(H_G)]
        s = {}
        for (off, k0, pb), h in work:
            q = qd_ref[pl.ds(off, Q_BLOCK), :]
            qh = jnp.where(hms[h], q, jnp.zeros_like(q))
            k = kd_ref[pl.ds(k0, off + Q_BLOCK - k0), :]
            b = bias_ref[h] if pb > 0 else bias_ref[h, :, Q_BLOCK:2 * Q_BLOCK]
            s[off, h] = _dot_nt(qh, k) + b
        pn, lse = {}, {}
        for (off, k0, pb), h in work:
            m = jnp.max(s[off, h], axis=-1, keepdims=True)
            p = jnp.exp(s[off, h] - m)
            den = jnp.sum(p, axis=-1, keepdims=True)
            pn[off, h] = (p / den).astype(BF16)
            lse[off, h] = m + jnp.log(den)
        oh = {}
        for (off, k0, pb), h in work:
            v = vd_ref[pl.ds(k0, off + Q_BLOCK - k0), :]
            oh[off, h] = jnp.dot(pn[off, h], v, preferred_element_type=F32)
        for off, k0, pb in items:
            o_acc, l_acc = oh[off, 0], lse[off, 0]
            for h in range(1, H_G):
                o_acc = jnp.where(hms[h], oh[off, h], o_acc)
                l_acc = jnp.where(hms[h], lse[off, h], l_acc)
            o_dst[pl.ds(off, Q_BLOCK), :] = o_acc
            l_dst[pl.ds(off, Q_BLOCK), :] = jnp.broadcast_to(l_acc, (Q_BLOCK, A_OUT))
    for r in range(dil):
        rows = pl.ds(r, lc, stride=dil) if dil > 1 else pl.ds(0, lc)
        src = pl.ds(r * lc, lc)
        for c in range(halves):
            cols = pl.ds(c * LANES, LANES)
            o_ref[c, rows, :] = od_ref[src, cols]
            lse_ref[c, rows, :] = ld_ref[src, cols]


def _attn_prompt(a_slabs, bias, g, B, L):
    dil = GROUPS[g][1]
    halves = A_OUT // LANES
    out_spec = pl.BlockSpec((halves, L, LANES), lambda b: (0, b, 0))
    dense = lambda dt: pltpu.VMEM((L, A_OUT), dt)
    return pl.pallas_call(
        functools.partial(_attn_prompt_kernel, dil=dil),
        grid=(B,),
        in_specs=[pl.BlockSpec((3 * halves, L, LANES), lambda b: (g, b, 0)),
                  pl.BlockSpec((H_G, Q_BLOCK, 2 * Q_BLOCK), lambda b: (0, 0, 0))],
        out_specs=[out_spec, out_spec],
        out_shape=[jax.ShapeDtypeStruct((halves, B * L, LANES), F32)] * 2,
        scratch_shapes=[dense(BF16), dense(BF16), dense(BF16), dense(F32), dense(F32)],
        compiler_params=_cparams(("parallel",)),
        name=f"attn_prompt_g{g}",
    )(a_slabs, bias)


def _delta_prompt_kernel(bq_ref, bk_ref, bv_ref, z_ref, bg_ref, cw_ref, hp_ref, nd_ref,
                         ob_ref, sout_ref, s_ref, xc_ref):
    c = pl.program_id(1)
    C = CHUNK
    n_seq = bq_ref.shape[0]

    @pl.when(c == 0)
    def _():
        s_ref[...] = jnp.zeros_like(s_ref)
        xc_ref[:, 0:8, :] = jnp.zeros((n_seq, 8, CONV_DIM), F32)

    xc_ref[:, 8:8 + C, 0:B_QK] = bq_ref[...]
    xc_ref[:, 8:8 + C, B_QK:2 * B_QK] = bk_ref[...]
    xc_ref[:, 8:8 + C, 2 * B_QK:CONV_DIM] = bv_ref[...]

    def conv(i, lo):
        y = cw_ref[0:1, lo:lo + DK] * xc_ref[i, 5:5 + C, lo:lo + DK]
        for j in range(1, CONV_W):
            y = y + cw_ref[j:j + 1, lo:lo + DK] * xc_ref[i, 5 + j:5 + j + C, lo:lo + DK]
        return _silu(y)

    hp = hp_ref[...]
    row = lax.broadcasted_iota(jnp.int32, (C, C), 0)
    colc = lax.broadcasted_iota(jnp.int32, (C, C), 1)
    tril = row >= colc
    tril_f = tril.astype(F32)
    eye = (row == colc).astype(F32)
    r2 = lax.broadcasted_iota(jnp.int32, (LANES, LANES), 0)
    c2 = lax.broadcasted_iota(jnp.int32, (LANES, LANES), 1)
    eye_l = (r2 == c2).astype(F32)
    nd = nd_ref[...]

    units = [(i, h) for i in range(n_seq) for h in range(HB)]
    idx = range(len(units))
    ks, kqs, vbs, kbgs, qgs, kdts, decays, elasts = [], [], [], [], [], [], [], []
    for i in range(n_seq):
        bg = bg_ref[i]
        beta_all = _sigmoid(bg)
        g_all = -jnp.exp(hp[0:1, :]) * _softplus(bg + hp[1:2, :])
        gc = _dot_f32(tril_f, g_all)
        gct = lax.dot_general(eye_l, gc, (((1,), (1,)), ((), ())), preferred_element_type=F32,
                              precision=lax.Precision.HIGHEST)
        for h in range(HB):
            lo = h * DK
            q = conv(i, lo)
            k = conv(i, B_QK + lo)
            v = conv(i, 2 * B_QK + lo)
            q = q * (lax.rsqrt(jnp.sum(q * q, axis=-1, keepdims=True) + EPS) * (DK ** -0.5))
            k = k * lax.rsqrt(jnp.sum(k * k, axis=-1, keepdims=True) + EPS)
            beta = beta_all[:, h:h + 1]
            gcol = gc[:, HB + h:HB + h + 1]
            grow = gct[HB + h:HB + h + 1, :]
            glast = gc[C - 1:C, HB + h:HB + h + 1]
            egc = jnp.exp(gcol)
            kb = k * beta
            ks.append(k.astype(BF16))
            kqs.append(jnp.concatenate([kb, q], axis=0).astype(BF16))
            vbs.append(v * beta)
            kbgs.append(kb * egc)
            qgs.append(q * egc)
            kdts.append((k * jnp.exp(glast - gcol)).T)
            decays.append(jnp.exp(jnp.where(tril, gcol - grow, NEG)))
            elasts.append(jnp.exp(glast))
    kq = [_dot_nt(kqs[u], ks[u]) for u in idx]
    pw = [jnp.where(row > colc, kq[u][0:C] * decays[u], 0.0) for u in idx]
    attn = [kq[u][C:2 * C] * decays[u] for u in idx]
    p = [eye - pw[u] for u in idx]
    pw = [_dot(pw[u], pw[u]) for u in idx]
    n = 4
    while n < C:
        nxt = [_dot(pw[u], pw[u]) for u in idx]
        p = [p[u] + _dot(p[u], pw[u]) for u in idx]
        pw = nxt
        n *= 2
    p = [p[u] + _dot(p[u], pw[u]) for u in idx]
    uw = [_dot(p[u], jnp.concatenate([vbs[u], kbgs[u]], axis=1)) for u in idx]
    s_old = [s_ref[i, h] for i, h in units]
    ws_qs = [_dot(jnp.concatenate([uw[u][:, DV:DV + DK], qgs[u]], axis=0), s_old[u]) for u in idx]
    v_new = [uw[u][:, 0:DV] - ws_qs[u][0:C] for u in idx]
    av = [_dot(jnp.concatenate([attn[u], kdts[u]], axis=0), v_new[u]) for u in idx]
    for u, (i, h) in enumerate(units):
        lo = h * DK
        s_ref[i, h] = s_old[u] * elasts[u] + av[u][C:C + DK]
        o = ws_qs[u][C:2 * C] + av[u][0:C]
        o = o * lax.rsqrt(jnp.mean(o * o, axis=-1, keepdims=True) + EPS) * nd
        ob_ref[i, :, lo:lo + DV] = (o * _silu(z_ref[i, :, lo:lo + DV])).astype(ob_ref.dtype)

    xc_ref[:, 0:8, :] = xc_ref[:, C:C + 8, :]

    @pl.when(c == pl.num_programs(1) - 1)
    def _():
        sout_ref[...] = s_ref[...]


DELTA_SEQS = 2


def _delta_prompt(b_qkv, zg, bg, conv_w, head_params, norm_delta, l, B, L):
    n = L // CHUNK
    ns = DELTA_SEQS if B % DELTA_SEQS == 0 else 1
    seq = lambda t: t.reshape(B, L, t.shape[-1])
    rows = lambda w, part: pl.BlockSpec((ns, CHUNK, w), lambda b, c: (b, c, part))
    o_b, s_fin = pl.pallas_call(
        _delta_prompt_kernel,
        grid=(B // ns, n),
        in_specs=[
            rows(B_QK, 0), rows(B_QK, 1), rows(B_QK, 2),
            rows(B_V, 0),
            rows(LANES, 0),
            pl.BlockSpec((None, CONV_W, CONV_DIM), lambda b, c: (l, 0, 0)),
            pl.BlockSpec((None, 2, LANES), lambda b, c: (l, 0, 0)),
            pl.BlockSpec((None, 1, DV), lambda b, c: (l, 0, 0)),
        ],
        out_specs=[
            rows(B_V, 0),
            pl.BlockSpec((ns, HB, DK, DV), lambda b, c: (b, 0, 0, 0)),
        ],
        out_shape=[
            jax.ShapeDtypeStruct((B, L, B_V), BF16),
            jax.ShapeDtypeStruct((B, HB, DK, DV), F32),
        ],
        scratch_shapes=[pltpu.VMEM((ns, HB, DK, DV), F32), pltpu.VMEM((ns, 8 + CHUNK, CONV_DIM), F32)],
        compiler_params=_cparams(("parallel", "arbitrary")),
        name="delta_prompt",
    )(seq(b_qkv), seq(b_qkv), seq(b_qkv), seq(zg), seq(bg), conv_w, head_params, norm_delta)
    return o_b.reshape(B * L, B_V), s_fin


def _merge_groups(os_, lses):
    m = jnp.maximum(jnp.maximum(lses[0], lses[1]), lses[2])
    es = [jnp.exp(t - m) for t in lses]
    den = es[0] + es[1] + es[2]
    return (es[0] * os_[0] + es[1] * os_[1] + es[2] * os_[2]) / den


def _mix_tail(o_a, ob_ref, ga_ref, gb_ref, x_ref, wa_ref, wb_ref, wo_ref, nw_ref, out_ref):
    ya = _dot(o_a, wa_ref[...])
    yb = jnp.dot(ob_ref[...], wb_ref[...], preferred_element_type=F32)
    mix = _sigmoid(ga_ref[...]) * ya + _sigmoid(gb_ref[...]) * yb
    y = _dot(mix, wo_ref[...])
    out_ref[...] = x_ref[...] + _rms(y, nw_ref[...])


def _mix_prompt_kernel(o0, l0, o1, l1, o2, l2, ob_ref, ga_ref, gb_ref, x_ref, wa_ref, wb_ref, wo_ref,
                       nw_ref, out_ref):
    wide = lambda r: jnp.concatenate([r[c] for c in range(A_OUT // LANES)], axis=1)
    o_a = _merge_groups([wide(o0), wide(o1), wide(o2)], [wide(l0), wide(l1), wide(l2)])
    _mix_tail(o_a, ob_ref, ga_ref, gb_ref, x_ref, wa_ref, wb_ref, wo_ref, nw_ref, out_ref)


def _mix_sample_kernel(oa_ref, ob_ref, ga_ref, gb_ref, x_ref, wa_ref, wb_ref, wo_ref, nw_ref, out_ref):
    _mix_tail(oa_ref[...], ob_ref, ga_ref, gb_ref, x_ref, wa_ref, wb_ref, wo_ref, nw_ref, out_ref)


def _mix(attn_parts, o_b, zg, x2d, wa, wb, wo, norm_w, l, tm):
    T = x2d.shape[0]
    rows = lambda w: pl.BlockSpec((tm, w), lambda i: (i, 0))
    weights = [
        pl.BlockSpec((None, A_OUT, D_MODEL), lambda i: (l, 0, 0)),
        pl.BlockSpec((None, B_V, D_MODEL), lambda i: (l, 0, 0)),
        pl.BlockSpec((None, D_MODEL, D_MODEL), lambda i: (l, 0, 0)),
        pl.BlockSpec((None, 1, D_MODEL), lambda i: (l, 0, 0)),
    ]
    common = [
        rows(B_V),
        pl.BlockSpec((tm, D_MODEL), lambda i: (i, 1)),
        pl.BlockSpec((tm, D_MODEL), lambda i: (i, 2)),
        rows(D_MODEL),
    ]
    if len(attn_parts) > 1:
        kern = _mix_prompt_kernel
        attn_specs = [pl.BlockSpec((A_OUT // LANES, tm, LANES), lambda i: (0, i, 0))] * len(attn_parts)
    else:
        kern = _mix_sample_kernel
        attn_specs = [rows(A_OUT)]
    return pl.pallas_call(
        kern,
        grid=(T // tm,),
        in_specs=attn_specs + common + weights,
        out_specs=rows(D_MODEL),
        out_shape=jax.ShapeDtypeStruct((T, D_MODEL), F32),
        compiler_params=_cparams(("parallel",)),
        name="mix",
    )(*attn_parts, o_b, zg, zg, x2d, wa, wb, wo, norm_w)


def _ffn_kernel(x_ref, nw1_ref, wg_ref, wu_ref, wd_ref, nw2_ref, out_ref, h_ref, acc_ref):
    j = pl.program_id(1)

    @pl.when(j == 0)
    def _():
        h_ref[...] = _rms(x_ref[...], nw1_ref[...]).astype(BF16)
        acc_ref[...] = jnp.zeros_like(acc_ref)

    h = h_ref[...]
    gt = jnp.dot(h, wg_ref[...], preferred_element_type=F32)
    up = jnp.dot(h, wu_ref[...], preferred_element_type=F32)
    acc_ref[...] += _dot(_silu(gt) * up, wd_ref[...])

    @pl.when(j == pl.num_programs(1) - 1)
    def _():
        out_ref[...] = x_ref[...] + _rms(acc_ref[...], nw2_ref[...])


def _ffn(x2d, nw1, w_in, w_out, nw2, l, tm):
    T = x2d.shape[0]
    nf = D_FF // FFN_TF
    return pl.pallas_call(
        _ffn_kernel,
        grid=(T // tm, nf),
        in_specs=[
            pl.BlockSpec((tm, D_MODEL), lambda i, j: (i, 0)),
            pl.BlockSpec((None, 1, D_MODEL), lambda i, j: (l, 0, 0)),
            pl.BlockSpec((None, D_MODEL, FFN_TF), lambda i, j: (l, 0, j)),
            pl.BlockSpec((None, D_MODEL, FFN_TF), lambda i, j: (l, 0, nf + j)),
            pl.BlockSpec((None, FFN_TF, D_MODEL), lambda i, j: (l, j, 0)),
            pl.BlockSpec((None, 1, D_MODEL), lambda i, j: (l, 0, 0)),
        ],
        out_specs=pl.BlockSpec((tm, D_MODEL), lambda i, j: (i, 0)),
        out_shape=jax.ShapeDtypeStruct((T, D_MODEL), F32),
        scratch_shapes=[pltpu.VMEM((tm, D_MODEL), BF16), pltpu.VMEM((tm, D_MODEL), F32)],
        compiler_params=_cparams(("parallel", "arbitrary")),
        name="ffn",
    )(x2d, nw1, w_in, w_in, w_out, nw2)


N_CACHED = N_KEYS - 1


def _attn_sample_kernel(a_ref, c0_ref, c1_ref, c2_ref, b0_ref, b1_ref, b2_ref, bnew_ref, seg_ref, o_ref):
    caches = (c0_ref, c1_ref, c2_ref)
    biases = (b0_ref, b1_ref, b2_ref)
    seg = seg_ref[...]
    a = a_ref[0]
    gh = [(g, h) for g in range(N_GROUPS) for h in range(H_G)]
    rows8 = lambda t: jnp.broadcast_to(t, (8, t.shape[-1]))
    q, vn, s_new = [], [], []
    for g in range(N_GROUPS):
        lo = g * A_GRP
        q.append(a[:, lo:lo + A_OUT] * ATTN_SCALE)
        vn.append(a[:, lo + 2 * A_OUT:lo + 3 * A_OUT])
        s_new.append(_dot(rows8(q[g] * a[:, lo + A_OUT:lo + 2 * A_OUT]), seg) + bnew_ref[g])
    head = lambda t, h: t[:, h * HD_A:(h + 1) * HD_A]
    s = {(g, h): _dot(rows8(head(q[g], h)), caches[g][0, h]) + biases[g][h:h + 1, :] for g, h in gh}
    pn, p_new, lse = {}, {}, {}
    for g, h in gh:
        sn = s_new[g][:, h * HD_A:h * HD_A + 1]
        m = jnp.maximum(jnp.max(s[g, h], axis=-1, keepdims=True), sn)
        p = jnp.exp(s[g, h] - m)
        e_new = jnp.exp(sn - m)
        den = jnp.sum(p, axis=-1, keepdims=True) + e_new
        pn[g, h] = p / den
        p_new[g, h] = e_new / den
        lse[g, h] = m + jnp.log(den)
    o = {(g, h): _dot_nt(pn[g, h], caches[g][1, h]) + p_new[g, h] * head(vn[g], h) for g, h in gh}
    merged = [_merge_groups([o[g, h] for g in range(N_GROUPS)], [lse[g, h] for g in range(N_GROUPS)])
              for h in range(H_G)]
    o_ref[0] = jnp.concatenate(merged, axis=1)[0:1, :]


def _attn_sample(a_qkv, caches_t, biases, bias_new, seg, l):
    Bd = a_qkv.shape[0]
    cache_specs = [pl.BlockSpec((None, None) + c.shape[2:], lambda i: (l, i, 0, 0, 0, 0)) for c in caches_t]
    bias_specs = [pl.BlockSpec(b.shape, lambda i: (0, 0)) for b in biases]
    o = pl.pallas_call(
        _attn_sample_kernel,
        grid=(Bd,),
        in_specs=[pl.BlockSpec((1, 1, A_QKV), lambda i: (i, 0, 0))] + cache_specs + bias_specs + [
            pl.BlockSpec((N_GROUPS, 1, A_OUT), lambda i: (0, 0, 0)),
            pl.BlockSpec((A_OUT, A_OUT), lambda i: (0, 0)),
        ],
        out_specs=pl.BlockSpec((1, 1, A_OUT), lambda i: (i, 0, 0)),
        out_shape=jax.ShapeDtypeStruct((Bd, 1, A_OUT), F32),
        compiler_params=_cparams(("parallel",)),
        name="attn_sample",
    )(a_qkv.reshape(Bd, 1, A_QKV), *caches_t, *biases, bias_new, seg)
    return o.reshape(Bd, A_OUT)


def _sample_bias(rel_bias):
    rows, new = [], []
    for g, (win, dil) in enumerate(GROUPS):
        vals = _group_bias(rel_bias, g)
        sel = vals[1:][::-1][:, None, :]
        skip = jnp.full((N_CACHED, dil - 1, H_G), NEG, F32)
        rows.append(jnp.concatenate([sel, skip], axis=1).reshape(N_CACHED * dil, H_G).T)
        new.append(jnp.repeat(vals[0:1], HD_A, axis=1))
    return rows, jnp.stack(new)


def _delta_sample_kernel(bn_ref, cb_ref, z_ref, bg_ref, s_ref, cw_ref, hp_ref, nd_ref, ob_ref, sout_ref):
    bb = bn_ref.shape[0]
    bg = bg_ref[...]
    hp = hp_ref[...]
    beta_all = _sigmoid(bg)
    g_all = -jnp.exp(hp[0:1, :]) * _softplus(bg + hp[1:2, :])
    eg_all = jnp.exp(g_all)
    nd = nd_ref[...]
    r8 = lax.broadcasted_iota(jnp.int32, (1, 8, 1), 1)
    r2 = lax.broadcasted_iota(jnp.int32, (DK, DK), 0)
    c2 = lax.broadcasted_iota(jnp.int32, (DK, DK), 1)
    eye = jnp.broadcast_to((r2 == c2).astype(BF16)[None], (bb, DK, DK))

    def conv(lo):
        y = cw_ref[CONV_W - 1:CONV_W, lo:lo + DK] * bn_ref[:, :, lo:lo + DK]
        for j in range(CONV_W - 1):
            y = y + cw_ref[j:j + 1, lo:lo + DK] * cb_ref[:, j:j + 1, lo:lo + DK]
        return _silu(y)

    for h in range(HB):
        lo = h * DK
        q = conv(lo)
        k = conv(B_QK + lo)
        v = conv(2 * B_QK + lo)
        q = q * lax.rsqrt(jnp.sum(q * q, axis=-1, keepdims=True) + EPS) * (DK ** -0.5)
        k = k * lax.rsqrt(jnp.sum(k * k, axis=-1, keepdims=True) + EPS)
        beta = beta_all[:, :, h:h + 1]
        eg = eg_all[:, :, HB + h:HB + h + 1]
        s = s_ref[:, h]
        w = k * (beta * eg)
        qg = q * eg
        lhs = jnp.where(r8 == 0, w, jnp.where(r8 == 1, qg, 0.0))
        r = jnp.einsum('bmk,bkv->bmv', lhs.astype(BF16), s.astype(BF16), preferred_element_type=F32)
        v_new = v * beta - r[:, 0:1, :]
        attn = jnp.sum(q * k, axis=-1, keepdims=True)
        o = r[:, 1:2, :] + attn * v_new
        kt = jnp.einsum('bij,bmj->bim', eye, jnp.broadcast_to(k, (bb, 8, DK)).astype(BF16),
                        preferred_element_type=F32)
        sout_ref[:, h] = s * eg + kt[:, :, 0:1] * v_new
        o = o * lax.rsqrt(jnp.mean(o * o, axis=-1, keepdims=True) + EPS) * nd
        ob_ref[:, :, lo:lo + DV] = (o * _silu(z_ref[:, :, lo:lo + DV])).astype(ob_ref.dtype)


def _delta_sample(b_qkv, conv_state, zg, bg, state, conv_w, head_params, norm_delta, l, bb):
    Bd = b_qkv.shape[0]
    row = lambda w: pl.BlockSpec((bb, 1, w), lambda i: (i, 0, 0))
    o_b, s_new = pl.pallas_call(
        _delta_sample_kernel,
        grid=(Bd // bb,),
        in_specs=[
            row(CONV_DIM),
            pl.BlockSpec((None, bb, CONV_W - 1, CONV_DIM), lambda i: (l, i, 0, 0)),
            row(B_V),
            row(LANES),
            pl.BlockSpec((None, bb, HB, DK, DV), lambda i: (l, i, 0, 0, 0)),
            pl.BlockSpec((None, CONV_W, CONV_DIM), lambda i: (l, 0, 0)),
            pl.BlockSpec((None, 2, LANES), lambda i: (l, 0, 0)),
            pl.BlockSpec((None, 1, DV), lambda i: (l, 0, 0)),
        ],
        out_specs=[
            row(B_V),
            pl.BlockSpec((bb, HB, DK, DV), lambda i: (i, 0, 0, 0)),
        ],
        out_shape=[
            jax.ShapeDtypeStruct((Bd, 1, B_V), BF16),
            jax.ShapeDtypeStruct((Bd, HB, DK, DV), F32),
        ],
        compiler_params=_cparams(("parallel",)),
        name="delta_sample",
    )(b_qkv.reshape(Bd, 1, CONV_DIM), conv_state, zg.reshape(Bd, 1, ZG_DIM), bg.reshape(Bd, 1, LANES),
      state, conv_w, head_params, norm_delta)
    return o_b.reshape(Bd, B_V), s_new


def _prepare(p):
    w_in = p['w_in']
    w_main = jnp.concatenate([w_in[:, :, :COL_BG], w_in[:, :, COL_GATE:]], axis=-1).astype(BF16)
    w_main = jnp.transpose(w_main.reshape(DEPTH, D_MODEL, N_MAIN // PROJ_TN, PROJ_TN), (0, 2, 1, 3))
    w_bg = jnp.pad(w_in[:, :, COL_BG:COL_GATE], ((0, 0), (0, 0), (0, LANES - 2 * HB))).astype(BF16)
    pad = lambda t: jnp.pad(t.astype(F32), ((0, 0), (HB, LANES - 2 * HB)))
    head_params = jnp.stack([pad(p['a_log']), pad(p['dt_bias'])], axis=1)
    vec = lambda t: t.astype(F32)[:, None, :]
    return dict(
        w_main=w_main, w_bg=w_bg, head_params=head_params,
        conv_w=p['conv_w'].astype(F32),
        wa=p['w_branch_a'].astype(BF16), wb=p['w_branch_b'].astype(BF16), wo=p['w_out'].astype(BF16),
        w_ffn_in=p['w_ffn_in'].astype(BF16), w_ffn_out=p['w_ffn_out'].astype(BF16),
        norm_pre_mix=vec(p['norm_pre_mix']), norm_post_mix=vec(p['norm_post_mix']),
        norm_pre_ffn=vec(p['norm_pre_ffn']), norm_post_ffn=vec(p['norm_post_ffn']),
        norm_delta=vec(p['norm_delta']),
    )


def _kv_rows(a_qkv, B, L, g, rows):
    t = a_qkv.reshape(B, L, N_GROUPS, 3, H_G, HD_A)
    return t[:, L - rows:, g, 1:]


def _trunk_prompt(x, w, rel_bias):
    B, L, _ = x.shape
    T = B * L
    tm = min(1024, T)
    x2d = x.reshape(T, D_MODEL)
    biases = [_prompt_bias(rel_bias, g) for g in range(N_GROUPS)]
    kv_out = [[] for _ in range(N_GROUPS)]
    s_out, conv_out = [], []
    for l in range(DEPTH):
        a_qkv, b_qkv, zg, bg, kvt = _in_proj(x2d, w['norm_pre_mix'], w['w_main'], w['w_bg'], l, tm, seq_len=L)
        parts = []
        for g in range(N_GROUPS):
            parts.extend(_attn_prompt(a_qkv, biases[g], g, B, L))
        o_b, s_fin = _delta_prompt(b_qkv, zg, bg, w['conv_w'], w['head_params'], w['norm_delta'], l, B, L)
        x2d = _mix(parts, o_b, zg, x2d, w['wa'], w['wb'], w['wo'], w['norm_post_mix'], l, min(256, T))
        x2d = _ffn(x2d, w['norm_pre_ffn'], w['w_ffn_in'], w['w_ffn_out'], w['norm_post_ffn'], l, min(512, T))
        for g, (win, _) in enumerate(GROUPS):
            rows = min(win, L)
            kv_out[g].append(kvt[:, g, :, L - rows:].reshape(B, 2, H_G, HD_A, rows))
        s_out.append(s_fin)
        conv_out.append(b_qkv.reshape(B, L, CONV_DIM)[:, L - (CONV_W - 1):])
    kv = [jnp.transpose(jnp.stack(t), (0, 1, 5, 2, 3, 4)) for t in kv_out]
    return (x2d.reshape(B, L, D_MODEL), kv[0], kv[1], kv[2], jnp.stack(s_out), jnp.stack(conv_out))


def _trunk_sample(x, caches, state, conv_state, w, rel_bias):
    Bd = x.shape[0]
    x2d = x.reshape(Bd, D_MODEL)
    biases, bias_new = _sample_bias(rel_bias)
    lane_head = np.arange(A_OUT) // HD_A
    seg = jnp.asarray(lane_head[:, None] == lane_head[None, :], BF16)
    bb = min(8, Bd)
    for c, (win, dil) in zip(caches, GROUPS):
        assert c.shape[2] == N_CACHED * dil, "cache length must equal the group's window"
    caches_t = [jnp.transpose(c, (0, 1, 3, 4, 5, 2)) for c in caches]
    kv_out = [[] for _ in range(N_GROUPS)]
    s_out, conv_out = [], []
    for l in range(DEPTH):
        a_qkv, b_qkv, zg, bg = _in_proj(x2d, w['norm_pre_mix'], w['w_main'], w['w_bg'], l, Bd)
        o_a = _attn_sample(a_qkv, caches_t, biases, bias_new, seg, l)
        o_b, s_new = _delta_sample(b_qkv, conv_state, zg, bg, state, w['conv_w'], w['head_params'],
                                   w['norm_delta'], l, bb)
        x2d = _mix([o_a], o_b, zg, x2d, w['wa'], w['wb'], w['wo'], w['norm_post_mix'], l, Bd)
        x2d = _ffn(x2d, w['norm_pre_ffn'], w['w_ffn_in'], w['w_ffn_out'], w['norm_post_ffn'], l, Bd)
        for g in range(N_GROUPS):
            kv_out[g].append(_kv_rows(a_qkv, Bd, 1, g, 1))
        s_out.append(s_new)
        conv_out.append(jnp.concatenate([conv_state[l][:, 1:], b_qkv[:, None, :]], axis=1))
    return (x2d.reshape(Bd, 1, D_MODEL), jnp.stack(kv_out[0]), jnp.stack(kv_out[1]), jnp.stack(kv_out[2]),
            jnp.stack(s_out), jnp.stack(conv_out))


def kernel(x_prompt, x_sample, cache_kv_w128, cache_kv_w512, cache_kv_w2048, state_delta, state_conv,
           rel_bias, norm_pre_mix, w_in, conv_w, a_log, dt_bias, norm_delta, w_branch_a, w_branch_b,
           w_out, norm_post_mix, norm_pre_ffn, w_ffn_in, w_ffn_out, norm_post_ffn):
    p = dict(w_in=w_in, conv_w=conv_w, a_log=a_log, dt_bias=dt_bias, norm_delta=norm_delta,
             w_branch_a=w_branch_a, w_branch_b=w_branch_b, w_out=w_out, norm_pre_mix=norm_pre_mix,
             norm_post_mix=norm_post_mix, norm_pre_ffn=norm_pre_ffn, w_ffn_in=w_ffn_in,
             w_ffn_out=w_ffn_out, norm_post_ffn=norm_post_ffn)
    w = _prepare(p)
    assert x_sample.shape[1] == 1, "the decode trunk handles one new token per sequence"
    y_p, kv0_p, kv1_p, kv2_p, s_p, conv_p = _trunk_prompt(x_prompt, w, rel_bias)
    y_s, kv0_s, kv1_s, kv2_s, s_s, conv_s = _trunk_sample(
        x_sample, (cache_kv_w128, cache_kv_w512, cache_kv_w2048), state_delta, state_conv, w, rel_bias)
    return (y_p, y_s, kv0_p, kv1_p, kv2_p, s_p, conv_p, kv0_s, kv1_s, kv2_s, s_s, conv_s)
```

```python
import functools

import numpy as np
import jax
import jax.numpy as jnp
from jax import lax
from jax.experimental import pallas as pl
from jax.experimental.pallas import tpu as pltpu

F32 = jnp.float32
BF16 = jnp.bfloat16

D_MODEL = 1024
DEPTH = 4
GROUPS = ((128, 1), (512, 4), (2048, 16))
N_GROUPS = 3
HD_A = 64
H_G = 4
A_OUT = H_G * HD_A
A_GRP = 3 * A_OUT
A_QKV = N_GROUPS * A_GRP
N_KEYS = 129
Q_BLOCK = 128
ATTN_SCALE = HD_A ** -0.5
N_BUCKETS = 32
REL_MAX_DIST = 2048
HB = 8
DK = 128
DV = 128
B_QK = HB * DK
B_V = HB * DV
CONV_W = 4
CONV_DIM = 2 * B_QK + B_V
CHUNK = 64
D_FF = 2816
EPS = 1e-6
NEG = -1e30

COL_B = A_QKV
COL_Z = COL_B + CONV_DIM
COL_BG = COL_Z + B_V
COL_GATE = COL_BG + 2 * HB
N_IN = COL_GATE + 2 * D_MODEL
ZG_DIM = B_V + 2 * D_MODEL
N_MAIN = A_QKV + CONV_DIM + ZG_DIM
PROJ_TN = 768
N_TILES_A = A_QKV // PROJ_TN
N_TILES_B = CONV_DIM // PROJ_TN
N_TILES_ZG = ZG_DIM // PROJ_TN
LANES = 128
FFN_TF = D_FF // 2

VMEM_LIMIT = 52 * 1024 * 1024


def _cparams(sem):
    return pltpu.CompilerParams(dimension_semantics=sem, vmem_limit_bytes=VMEM_LIMIT)


def _sigmoid(x):
    return 1.0 / (1.0 + jnp.exp(-x))


def _silu(x):
    return x * _sigmoid(x)


def _softplus(x):
    return jnp.maximum(x, 0.0) + jnp.log(1.0 + jnp.exp(-jnp.abs(x)))


def _rms(x, w):
    return x * lax.rsqrt(jnp.mean(x * x, axis=-1, keepdims=True) + EPS) * w


def _dot(a, b):
    return jnp.dot(a.astype(BF16), b.astype(BF16), preferred_element_type=F32)


def _dot_nt(a, b):
    return lax.dot_general(a.astype(BF16), b.astype(BF16), (((1,), (1,)), ((), ())),
                           preferred_element_type=F32)


def _dot_f32(a, b):
    return jnp.dot(a, b, preferred_element_type=F32, precision=lax.Precision.HIGHEST)


def _in_proj_kernel(x_ref, nw_ref, w_ref, wbg_ref, *refs, prompt):
    if prompt:
        oa_refs, (ob_ref, ozg_ref, obg_ref, kvt_ref, h_ref, slab_ref) = refs[:N_GROUPS], refs[N_GROUPS:]
    else:
        oa_ref, ob_ref, ozg_ref, obg_ref, h_ref = refs
    j = pl.program_id(1)
    tm = x_ref.shape[0]
    n_slabs = PROJ_TN // LANES

    @pl.when(j == 0)
    def _():
        hb = _rms(x_ref[...], nw_ref[...]).astype(BF16)
        h_ref[...] = hb
        obg_ref[...] = jnp.dot(hb, wbg_ref[...], preferred_element_type=F32)

    def group_tile(g):
        acc = jnp.dot(h_ref[...], w_ref[...], preferred_element_type=F32)
        if not prompt:
            oa_ref[...] = acc
            return
        kvt_ref[...] = acc[:, A_OUT:].T
        dil = GROUPS[g][1]
        qkv = jnp.concatenate([acc[:, :A_OUT] * ATTN_SCALE, acc[:, A_OUT:]], axis=1)
        if dil == 1:
            oa_refs[g][0] = qkv.astype(BF16)
            return
        for c in range(n_slabs):
            slab_ref[c] = qkv[:, c * LANES:(c + 1) * LANES]
        for r in range(dil):
            for c in range(n_slabs):
                rows = slab_ref[c, pl.ds(r, tm // dil, stride=dil), :]
                oa_refs[g][r, :, c * LANES:(c + 1) * LANES] = rows.astype(BF16)

    for g in range(N_TILES_A):
        pl.when(j == g)(functools.partial(group_tile, g))

    @pl.when((j >= N_TILES_A) & (j < N_TILES_A + N_TILES_B))
    def _():
        ob_ref[...] = jnp.dot(h_ref[...], w_ref[...], preferred_element_type=F32)

    @pl.when(j >= N_TILES_A + N_TILES_B)
    def _():
        ozg_ref[...] = jnp.dot(h_ref[...], w_ref[...], preferred_element_type=F32)


def _in_proj(x2d, norm_w, w_main, w_bg, l, tm, seq_len=None):
    assert A_GRP == PROJ_TN, "one column tile per attention group"
    T = x2d.shape[0]
    n_tiles = N_TILES_A + N_TILES_B + N_TILES_ZG
    last_a, first_zg = N_TILES_A - 1, N_TILES_A + N_TILES_B
    prompt = seq_len is not None
    scratch = [pltpu.VMEM((tm, D_MODEL), BF16)]
    if prompt:
        per_seq = seq_len // tm
        n_seq = T // seq_len
        a_specs = [pl.BlockSpec((None, dil, tm // dil, A_GRP), lambda i, j: (i // per_seq, 0, i % per_seq, 0))
                   for _, dil in GROUPS]
        a_shapes = [jax.ShapeDtypeStruct((n_seq, dil, seq_len // dil, A_GRP), BF16) for _, dil in GROUPS]
        extra_specs = [pl.BlockSpec((None, None, 2 * A_OUT, tm),
                                    lambda i, j: (i // per_seq, jnp.minimum(j, last_a), 0, i % per_seq))]
        extra_shapes = [jax.ShapeDtypeStruct((n_seq, N_GROUPS, 2 * A_OUT, seq_len), F32)]
        scratch.append(pltpu.VMEM((PROJ_TN // LANES, tm, LANES), F32))
    else:
        a_specs = [pl.BlockSpec((tm, PROJ_TN), lambda i, j: (i, jnp.minimum(j, last_a)))]
        a_shapes = [jax.ShapeDtypeStruct((T, A_QKV), F32)]
        extra_specs, extra_shapes = [], []
    return pl.pallas_call(
        functools.partial(_in_proj_kernel, prompt=prompt),
        grid=(T // tm, n_tiles),
        in_specs=[
            pl.BlockSpec((tm, D_MODEL), lambda i, j: (i, 0)),
            pl.BlockSpec((None, 1, D_MODEL), lambda i, j: (l, 0, 0)),
            pl.BlockSpec((None, None, D_MODEL, PROJ_TN), lambda i, j: (l, j, 0, 0)),
            pl.BlockSpec((None, D_MODEL, LANES), lambda i, j: (l, 0, 0)),
        ],
        out_specs=a_specs + [
            pl.BlockSpec((tm, PROJ_TN), lambda i, j: (i, jnp.clip(j - N_TILES_A, 0, N_TILES_B - 1))),
            pl.BlockSpec((tm, PROJ_TN), lambda i, j: (i, jnp.maximum(j - first_zg, 0))),
            pl.BlockSpec((tm, LANES), lambda i, j: (i, 0)),
        ] + extra_specs,
        out_shape=a_shapes + [
            jax.ShapeDtypeStruct((T, CONV_DIM), F32),
            jax.ShapeDtypeStruct((T, ZG_DIM), F32),
            jax.ShapeDtypeStruct((T, LANES), F32),
        ] + extra_shapes,
        scratch_shapes=scratch,
        compiler_params=_cparams(("parallel", "arbitrary")),
        name="in_proj",
    )(x2d, norm_w, w_main, w_bg)


def _t5_bucket(dist):
    max_exact = N_BUCKETS // 2
    d = np.maximum(dist, 1).astype(np.float32)
    large = max_exact + (np.log(d / max_exact) / np.log(REL_MAX_DIST / max_exact)
                         * (N_BUCKETS - max_exact)).astype(np.int32)
    large = np.minimum(large, N_BUCKETS - 1)
    return np.where(dist < max_exact, dist, large).astype(np.int32)


def _group_bias(rel_bias, g):
    dil = GROUPS[g][1]
    bucket = _t5_bucket(np.arange(N_KEYS) * dil)
    return rel_bias[bucket][:, g * H_G:(g + 1) * H_G].astype(F32)


def _prompt_bias(rel_bias, g):
    period = 3 * Q_BLOCK
    vals = _group_bias(rel_bias, g)
    u = jnp.concatenate([vals[::-1], jnp.full((period - N_KEYS, H_G), NEG, F32)], axis=0).T
    flat = jnp.tile(u, (1, Q_BLOCK))[:, :Q_BLOCK * (period - 1)]
    return flat.reshape(H_G, Q_BLOCK, period - 1)[:, :, :2 * Q_BLOCK]


def _attn_group(g, a_ref, bias_ref, emit):
    dil, lc, _ = a_ref.shape
    nb = lc // Q_BLOCK
    lane_head = lax.broadcasted_iota(jnp.int32, (1, A_OUT), 1) // HD_A
    hms = [lane_head == h for h in range(H_G)]
    blocks = [(r, pb) for r in range(dil) for pb in range(nb)]
    for i in range(0, len(blocks), 2):
        items = blocks[i:i + 2]
        work = [(it, h) for it in items for h in range(H_G)]
        keys = lambda r, pb, part: a_ref[r, pl.ds((pb - 1) * Q_BLOCK, 2 * Q_BLOCK) if pb > 0
                                         else pl.ds(0, Q_BLOCK), part * A_OUT:(part + 1) * A_OUT]
        s = {}
        for (r, pb), h in work:
            q = a_ref[r, pl.ds(pb * Q_BLOCK, Q_BLOCK), 0:A_OUT]
            qh = jnp.where(hms[h], q, jnp.zeros_like(q))
            b = bias_ref[h] if pb > 0 else bias_ref[h, :, Q_BLOCK:2 * Q_BLOCK]
            s[r, pb, h] = _dot_nt(qh, keys(r, pb, 1)) + b
        pn, lse = {}, {}
        for (r, pb), h in work:
            m = jnp.max(s[r, pb, h], axis=-1, keepdims=True)
            p = jnp.exp(s[r, pb, h] - m)
            den = jnp.sum(p, axis=-1, keepdims=True)
            pn[r, pb, h] = (p / den).astype(BF16)
            lse[r, pb, h] = m + jnp.log(den)
        oh = {}
        for (r, pb), h in work:
            oh[r, pb, h] = jnp.dot(pn[r, pb, h], keys(r, pb, 2), preferred_element_type=F32)
        for r, pb in items:
            o_acc, l_acc = oh[r, pb, 0], lse[r, pb, 0]
            for h in range(1, H_G):
                o_acc = jnp.where(hms[h], oh[r, pb, h], o_acc)
                l_acc = jnp.where(hms[h], lse[r, pb, h], l_acc)
            emit(r, pb, o_acc, jnp.broadcast_to(l_acc, (Q_BLOCK, A_OUT)))


def _attn_prompt_kernel(a0_ref, a1_ref, a2_ref, b0_ref, b1_ref, b2_ref, o_ref,
                        od_ref, ld_ref, orun_ref, lrun_ref):
    L = o_ref.shape[0]
    halves = A_OUT // LANES
    for g, (a_ref, bias_ref) in enumerate(((a0_ref, b0_ref), (a1_ref, b1_ref), (a2_ref, b2_ref))):
        dil = a_ref.shape[0]
        lc = L // dil
        if dil == 1:
            assert g == 0

            def emit(r, pb, o, lse):
                for c in range(halves):
                    orun_ref[c, pl.ds(pb * Q_BLOCK, Q_BLOCK), :] = o[:, c * LANES:(c + 1) * LANES]
                    lrun_ref[c, pl.ds(pb * Q_BLOCK, Q_BLOCK), :] = lse[:, c * LANES:(c + 1) * LANES]

            _attn_group(g, a_ref, bias_ref, emit)
            continue

        def emit(r, pb, o, lse, lc=lc):
            od_ref[pl.ds(r * lc + pb * Q_BLOCK, Q_BLOCK), :] = o
            ld_ref[pl.ds(r * lc + pb * Q_BLOCK, Q_BLOCK), :] = lse

        _attn_group(g, a_ref, bias_ref, emit)
        for r in range(dil):
            rows = pl.ds(r, lc, stride=dil)
            src = pl.ds(r * lc, lc)
            for c in range(halves):
                cols = pl.ds(c * LANES, LANES)
                o_new, l_new = od_ref[src, cols], ld_ref[src, cols]
                o_old, l_old = orun_ref[c, rows, :], lrun_ref[c, rows, :]
                m = jnp.maximum(l_old, l_new)
                e_old, e_new = jnp.exp(l_old - m), jnp.exp(l_new - m)
                den = e_old + e_new
                orun_ref[c, rows, :] = (e_old * o_old + e_new * o_new) / den
                if g < N_GROUPS - 1:
                    lrun_ref[c, rows, :] = m + jnp.log(den)
    for c in range(halves):
        o_ref[:, c * LANES:(c + 1) * LANES] = orun_ref[c]


def _attn_prompt(a_groups, biases, B, L):
    halves = A_OUT // LANES
    a_specs = [pl.BlockSpec((None,) + a.shape[1:], lambda b: (b, 0, 0, 0)) for a in a_groups]
    bias_spec = pl.BlockSpec((H_G, Q_BLOCK, 2 * Q_BLOCK), lambda b: (0, 0, 0))
    return pl.pallas_call(
        _attn_prompt_kernel,
        grid=(B,),
        in_specs=a_specs + [bias_spec] * N_GROUPS,
        out_specs=pl.BlockSpec((L, A_OUT), lambda b: (b, 0)),
        out_shape=jax.ShapeDtypeStruct((B * L, A_OUT), F32),
        scratch_shapes=[pltpu.VMEM((L, A_OUT), F32), pltpu.VMEM((L, A_OUT), F32),
                        pltpu.VMEM((halves, L, LANES), F32), pltpu.VMEM((halves, L, LANES), F32)],
        compiler_params=_cparams(("parallel",)),
        name="attn_prompt",
    )(*a_groups, *biases)


def _delta_prompt_kernel(bq_ref, bk_ref, bv_ref, z_ref, bg_ref, cw_ref, hp_ref, nd_ref,
                         ob_ref, sout_ref, s_ref, xc_ref):
    c = pl.program_id(1)
    C = CHUNK
    n_seq = bq_ref.shape[0]

    @pl.when(c == 0)
    def _():
        s_ref[...] = jnp.zeros_like(s_ref)
        xc_ref[:, 0:8, :] = jnp.zeros((n_seq, 8, CONV_DIM), F32)

    xc_ref[:, 8:8 + C, 0:B_QK] = bq_ref[...]
    xc_ref[:, 8:8 + C, B_QK:2 * B_QK] = bk_ref[...]
    xc_ref[:, 8:8 + C, 2 * B_QK:CONV_DIM] = bv_ref[...]

    def conv(i, lo):
        y = cw_ref[0:1, lo:lo + DK] * xc_ref[i, 5:5 + C, lo:lo + DK]
        for j in range(1, CONV_W):
            y = y + cw_ref[j:j + 1, lo:lo + DK] * xc_ref[i, 5 + j:5 + j + C, lo:lo + DK]
        return _silu(y)

    hp = hp_ref[...]
    row = lax.broadcasted_iota(jnp.int32, (C, C), 0)
    colc = lax.broadcasted_iota(jnp.int32, (C, C), 1)
    tril = row >= colc
    tril_f = tril.astype(F32)
    eye = (row == colc).astype(F32)
    r2 = lax.broadcasted_iota(jnp.int32, (LANES, LANES), 0)
    c2 = lax.broadcasted_iota(jnp.int32, (LANES, LANES), 1)
    eye_l = (r2 == c2).astype(F32)
    nd = nd_ref[...]

    units = [(i, h) for i in range(n_seq) for h in range(HB)]
    idx = range(len(units))
    ks, kqs, vbs, kbgs, qgs, kdts, decays, elasts = [], [], [], [], [], [], [], []
    for i in range(n_seq):
        bg = bg_ref[i]
        beta_all = _sigmoid(bg)
        g_all = -jnp.exp(hp[0:1, :]) * _softplus(bg + hp[1:2, :])
        gc = _dot_f32(tril_f, g_all)
        gct = lax.dot_general(eye_l, gc, (((1,), (1,)), ((), ())), preferred_element_type=F32,
                              precision=lax.Precision.HIGHEST)
        for h in range(HB):
            lo = h * DK
            q = conv(i, lo)
            k = conv(i, B_QK + lo)
            v = conv(i, 2 * B_QK + lo)
            q = q * (lax.rsqrt(jnp.sum(q * q, axis=-1, keepdims=True) + EPS) * (DK ** -0.5))
            k = k * lax.rsqrt(jnp.sum(k * k, axis=-1, keepdims=True) + EPS)
            beta = beta_all[:, h:h + 1]
            gcol = gc[:, HB + h:HB + h + 1]
            grow = gct[HB + h:HB + h + 1, :]
            glast = gc[C - 1:C, HB + h:HB + h + 1]
            egc = jnp.exp(gcol)
            kb = k * beta
            ks.append(k.astype(BF16))
            kqs.append(jnp.concatenate([kb, q], axis=0).astype(BF16))
            vbs.append(v * beta)
            kbgs.append(kb * egc)
            qgs.append(q * egc)
            kdts.append((k * jnp.exp(glast - gcol)).T)
            decays.append(jnp.exp(jnp.where(tril, gcol - grow, NEG)))
            elasts.append(jnp.exp(glast))
    kq = [_dot_nt(kqs[u], ks[u]) for u in idx]
    pw = [jnp.where(row > colc, kq[u][0:C] * decays[u], 0.0) for u in idx]
    attn = [kq[u][C:2 * C] * decays[u] for u in idx]
    p = [eye - pw[u] for u in idx]
    pw = [_dot(pw[u], pw[u]) for u in idx]
    n = 4
    while n < C:
        nxt = [_dot(pw[u], pw[u]) for u in idx]
        p = [p[u] + _dot(p[u], pw[u]) for u in idx]
        pw = nxt
        n *= 2
    p = [p[u] + _dot(p[u], pw[u]) for u in idx]
    uw = [_dot(p[u], jnp.concatenate([vbs[u], kbgs[u]], axis=1)) for u in idx]
    s_old = [s_ref[i, h] for i, h in units]
    ws_qs = [_dot(jnp.concatenate([uw[u][:, DV:DV + DK], qgs[u]], axis=0), s_old[u]) for u in idx]
    v_new = [uw[u][:, 0:DV] - ws_qs[u][0:C] for u in idx]
    av = [_dot(jnp.concatenate([attn[u], kdts[u]], axis=0), v_new[u]) for u in idx]
    for u, (i, h) in enumerate(units):
        lo = h * DK
        s_ref[i, h] = s_old[u] * elasts[u] + av[u][C:C + DK]
        o = ws_qs[u][C:2 * C] + av[u][0:C]
        o = o * lax.rsqrt(jnp.mean(o * o, axis=-1, keepdims=True) + EPS) * nd
        ob_ref[i, :, lo:lo + DV] = (o * _silu(z_ref[i, :, lo:lo + DV])).astype(ob_ref.dtype)

    xc_ref[:, 0:8, :] = xc_ref[:, C:C + 8, :]

    @pl.when(c == pl.num_programs(1) - 1)
    def _():
        sout_ref[...] = s_ref[...]


DELTA_SEQS = 2


def _delta_prompt(b_qkv, zg, bg, conv_w, head_params, norm_delta, l, B, L):
    n = L // CHUNK
    ns = DELTA_SEQS if B % DELTA_SEQS == 0 else 1
    seq = lambda t: t.reshape(B, L, t.shape[-1])
    rows = lambda w, part: pl.BlockSpec((ns, CHUNK, w), lambda b, c: (b, c, part))
    o_b, s_fin = pl.pallas_call(
        _delta_prompt_kernel,
        grid=(B // ns, n),
        in_specs=[
            rows(B_QK, 0), rows(B_QK, 1), rows(B_QK, 2),
            rows(B_V, 0),
            rows(LANES, 0),
            pl.BlockSpec((None, CONV_W, CONV_DIM), lambda b, c: (l, 0, 0)),
            pl.BlockSpec((None, 2, LANES), lambda b, c: (l, 0, 0)),
            pl.BlockSpec((None, 1, DV), lambda b, c: (l, 0, 0)),
        ],
        out_specs=[
            rows(B_V, 0),
            pl.BlockSpec((ns, HB, DK, DV), lambda b, c: (b, 0, 0, 0)),
        ],
        out_shape=[
            jax.ShapeDtypeStruct((B, L, B_V), BF16),
            jax.ShapeDtypeStruct((B, HB, DK, DV), F32),
        ],
        scratch_shapes=[pltpu.VMEM((ns, HB, DK, DV), F32), pltpu.VMEM((ns, 8 + CHUNK, CONV_DIM), F32)],
        compiler_params=_cparams(("parallel", "arbitrary")),
        name="delta_prompt",
    )(seq(b_qkv), seq(b_qkv), seq(b_qkv), seq(zg), seq(bg), conv_w, head_params, norm_delta)
    return o_b.reshape(B * L, B_V), s_fin


def _merge_groups(os_, lses):
    m = jnp.maximum(jnp.maximum(lses[0], lses[1]), lses[2])
    es = [jnp.exp(t - m) for t in lses]
    den = es[0] + es[1] + es[2]
    return (es[0] * os_[0] + es[1] * os_[1] + es[2] * os_[2]) / den


def _mix_kernel(oa_ref, ob_ref, ga_ref, gb_ref, x_ref, wa_ref, wb_ref, wo_ref, nw_ref, out_ref):
    ya = _dot(oa_ref[...], wa_ref[...])
    yb = jnp.dot(ob_ref[...], wb_ref[...], preferred_element_type=F32)
    mix = _sigmoid(ga_ref[...]) * ya + _sigmoid(gb_ref[...]) * yb
    y = _dot(mix, wo_ref[...])
    out_ref[...] = x_ref[...] + _rms(y, nw_ref[...])


def _mix(o_a, o_b, zg, x2d, wa, wb, wo, norm_w, l, tm):
    T = x2d.shape[0]
    rows = lambda w: pl.BlockSpec((tm, w), lambda i: (i, 0))
    return pl.pallas_call(
        _mix_kernel,
        grid=(T // tm,),
        in_specs=[
            rows(A_OUT),
            rows(B_V),
            pl.BlockSpec((tm, D_MODEL), lambda i: (i, 1)),
            pl.BlockSpec((tm, D_MODEL), lambda i: (i, 2)),
            rows(D_MODEL),
            pl.BlockSpec((None, A_OUT, D_MODEL), lambda i: (l, 0, 0)),
            pl.BlockSpec((None, B_V, D_MODEL), lambda i: (l, 0, 0)),
            pl.BlockSpec((None, D_MODEL, D_MODEL), lambda i: (l, 0, 0)),
            pl.BlockSpec((None, 1, D_MODEL), lambda i: (l, 0, 0)),
        ],
        out_specs=rows(D_MODEL),
        out_shape=jax.ShapeDtypeStruct((T, D_MODEL), F32),
        compiler_params=_cparams(("parallel",)),
        name="mix",
    )(o_a, o_b, zg, zg, x2d, wa, wb, wo, norm_w)


def _ffn_kernel(x_ref, nw1_ref, wg_ref, wu_ref, wd_ref, nw2_ref, out_ref, h_ref, acc_ref):
    j = pl.program_id(1)

    @pl.when(j == 0)
    def _():
        h_ref[...] = _rms(x_ref[...], nw1_ref[...]).astype(BF16)
        acc_ref[...] = jnp.zeros_like(acc_ref)

    h = h_ref[...]
    gt = jnp.dot(h, wg_ref[...], preferred_element_type=F32)
    up = jnp.dot(h, wu_ref[...], preferred_element_type=F32)
    acc_ref[...] += _dot(_silu(gt) * up, wd_ref[...])

    @pl.when(j == pl.num_programs(1) - 1)
    def _():
        out_ref[...] = x_ref[...] + _rms(acc_ref[...], nw2_ref[...])


def _ffn(x2d, nw1, w_in, w_out, nw2, l, tm):
    T = x2d.shape[0]
    nf = D_FF // FFN_TF
    return pl.pallas_call(
        _ffn_kernel,
        grid=(T // tm, nf),
        in_specs=[
            pl.BlockSpec((tm, D_MODEL), lambda i, j: (i, 0)),
            pl.BlockSpec((None, 1, D_MODEL), lambda i, j: (l, 0, 0)),
            pl.BlockSpec((None, D_MODEL, FFN_TF), lambda i, j: (l, 0, j)),
            pl.BlockSpec((None, D_MODEL, FFN_TF), lambda i, j: (l, 0, nf + j)),
            pl.BlockSpec((None, FFN_TF, D_MODEL), lambda i, j: (l, j, 0)),
            pl.BlockSpec((None, 1, D_MODEL), lambda i, j: (l, 0, 0)),
        ],
        out_specs=pl.BlockSpec((tm, D_MODEL), lambda i, j: (i, 0)),
        out_shape=jax.ShapeDtypeStruct((T, D_MODEL), F32),
        scratch_shapes=[pltpu.VMEM((tm, D_MODEL), BF16), pltpu.VMEM((tm, D_MODEL), F32)],
        compiler_params=_cparams(("parallel", "arbitrary")),
        name="ffn",
    )(x2d, nw1, w_in, w_in, w_out, nw2)


N_CACHED = N_KEYS - 1


def _attn_sample_kernel(a_ref, c0_ref, c1_ref, c2_ref, b0_ref, b1_ref, b2_ref, bnew_ref, seg_ref, o_ref):
    caches = (c0_ref, c1_ref, c2_ref)
    biases = (b0_ref, b1_ref, b2_ref)
    seg = seg_ref[...]
    a = a_ref[0]
    gh = [(g, h) for g in range(N_GROUPS) for h in range(H_G)]
    rows8 = lambda t: jnp.broadcast_to(t, (8, t.shape[-1]))
    q, vn, s_new = [], [], []
    for g in range(N_GROUPS):
        lo = g * A_GRP
        q.append(a[:, lo:lo + A_OUT] * ATTN_SCALE)
        vn.append(a[:, lo + 2 * A_OUT:lo + 3 * A_OUT])
        s_new.append(_dot(rows8(q[g] * a[:, lo + A_OUT:lo + 2 * A_OUT]), seg) + bnew_ref[g])
    head = lambda t, h: t[:, h * HD_A:(h + 1) * HD_A]
    s = {(g, h): _dot(rows8(head(q[g], h)), caches[g][0, h]) + biases[g][h:h + 1, :] for g, h in gh}
    pn, p_new, lse = {}, {}, {}
    for g, h in gh:
        sn = s_new[g][:, h * HD_A:h * HD_A + 1]
        m = jnp.maximum(jnp.max(s[g, h], axis=-1, keepdims=True), sn)
        p = jnp.exp(s[g, h] - m)
        e_new = jnp.exp(sn - m)
        den = jnp.sum(p, axis=-1, keepdims=True) + e_new
        pn[g, h] = p / den
        p_new[g, h] = e_new / den
        lse[g, h] = m + jnp.log(den)
    o = {(g, h): _dot_nt(pn[g, h], caches[g][1, h]) + p_new[g, h] * head(vn[g], h) for g, h in gh}
    merged = [_merge_groups([o[g, h] for g in range(N_GROUPS)], [lse[g, h] for g in range(N_GROUPS)])
              for h in range(H_G)]
    o_ref[0] = jnp.concatenate(merged, axis=1)[0:1, :]


def _attn_sample(a_qkv, caches_t, biases, bias_new, seg, l):
    Bd = a_qkv.shape[0]
    cache_specs = [pl.BlockSpec((None, None) + c.shape[2:], lambda i: (l, i, 0, 0, 0, 0)) for c in caches_t]
    bias_specs = [pl.BlockSpec(b.shape, lambda i: (0, 0)) for b in biases]
    o = pl.pallas_call(
        _attn_sample_kernel,
        grid=(Bd,),
        in_specs=[pl.BlockSpec((1, 1, A_QKV), lambda i: (i, 0, 0))] + cache_specs + bias_specs + [
            pl.BlockSpec((N_GROUPS, 1, A_OUT), lambda i: (0, 0, 0)),
            pl.BlockSpec((A_OUT, A_OUT), lambda i: (0, 0)),
        ],
        out_specs=pl.BlockSpec((1, 1, A_OUT), lambda i: (i, 0, 0)),
        out_shape=jax.ShapeDtypeStruct((Bd, 1, A_OUT), F32),
        compiler_params=_cparams(("parallel",)),
        name="attn_sample",
    )(a_qkv.reshape(Bd, 1, A_QKV), *caches_t, *biases, bias_new, seg)
    return o.reshape(Bd, A_OUT)


def _sample_bias(rel_bias):
    rows, new = [], []
    for g, (win, dil) in enumerate(GROUPS):
        vals = _group_bias(rel_bias, g)
        sel = vals[1:][::-1][:, None, :]
        skip = jnp.full((N_CACHED, dil - 1, H_G), NEG, F32)
        rows.append(jnp.concatenate([sel, skip], axis=1).reshape(N_CACHED * dil, H_G).T)
        new.append(jnp.repeat(vals[0:1], HD_A, axis=1))
    return rows, jnp.stack(new)


def _delta_sample_kernel(bn_ref, cb_ref, z_ref, bg_ref, s_ref, cw_ref, hp_ref, nd_ref, ob_ref, sout_ref):
    bb = bn_ref.shape[0]
    bg = bg_ref[...]
    hp = hp_ref[...]
    beta_all = _sigmoid(bg)
    g_all = -jnp.exp(hp[0:1, :]) * _softplus(bg + hp[1:2, :])
    eg_all = jnp.exp(g_all)
    nd = nd_ref[...]
    r8 = lax.broadcasted_iota(jnp.int32, (1, 8, 1), 1)
    r2 = lax.broadcasted_iota(jnp.int32, (DK, DK), 0)
    c2 = lax.broadcasted_iota(jnp.int32, (DK, DK), 1)
    eye = jnp.broadcast_to((r2 == c2).astype(BF16)[None], (bb, DK, DK))

    def conv(lo):
        y = cw_ref[CONV_W - 1:CONV_W, lo:lo + DK] * bn_ref[:, :, lo:lo + DK]
        for j in range(CONV_W - 1):
            y = y + cw_ref[j:j + 1, lo:lo + DK] * cb_ref[:, j:j + 1, lo:lo + DK]
        return _silu(y)

    for h in range(HB):
        lo = h * DK
        q = conv(lo)
        k = conv(B_QK + lo)
        v = conv(2 * B_QK + lo)
        q = q * lax.rsqrt(jnp.sum(q * q, axis=-1, keepdims=True) + EPS) * (DK ** -0.5)
        k = k * lax.rsqrt(jnp.sum(k * k, axis=-1, keepdims=True) + EPS)
        beta = beta_all[:, :, h:h + 1]
        eg = eg_all[:, :, HB + h:HB + h + 1]
        s = s_ref[:, h]
        w = k * (beta * eg)
        qg = q * eg
        lhs = jnp.where(r8 == 0, w, jnp.where(r8 == 1, qg, 0.0))
        r = jnp.einsum('bmk,bkv->bmv', lhs.astype(BF16), s.astype(BF16), preferred_element_type=F32)
        v_new = v * beta - r[:, 0:1, :]
        attn = jnp.sum(q * k, axis=-1, keepdims=True)
        o = r[:, 1:2, :] + attn * v_new
        kt = jnp.einsum('bij,bmj->bim', eye, jnp.broadcast_to(k, (bb, 8, DK)).astype(BF16),
                        preferred_element_type=F32)
        sout_ref[:, h] = s * eg + kt[:, :, 0:1] * v_new
        o = o * lax.rsqrt(jnp.mean(o * o, axis=-1, keepdims=True) + EPS) * nd
        ob_ref[:, :, lo:lo + DV] = (o * _silu(z_ref[:, :, lo:lo + DV])).astype(ob_ref.dtype)


def _delta_sample(b_qkv, conv_state, zg, bg, state, conv_w, head_params, norm_delta, l, bb):
    Bd = b_qkv.shape[0]
    row = lambda w: pl.BlockSpec((bb, 1, w), lambda i: (i, 0, 0))
    o_b, s_new = pl.pallas_call(
        _delta_sample_kernel,
        grid=(Bd // bb,),
        in_specs=[
            row(CONV_DIM),
            pl.BlockSpec((None, bb, CONV_W - 1, CONV_DIM), lambda i: (l, i, 0, 0)),
            row(B_V),
            row(LANES),
            pl.BlockSpec((None, bb, HB, DK, DV), lambda i: (l, i, 0, 0, 0)),
            pl.BlockSpec((None, CONV_W, CONV_DIM), lambda i: (l, 0, 0)),
            pl.BlockSpec((None, 2, LANES), lambda i: (l, 0, 0)),
            pl.BlockSpec((None, 1, DV), lambda i: (l, 0, 0)),
        ],
        out_specs=[
            row(B_V),
            pl.BlockSpec((bb, HB, DK, DV), lambda i: (i, 0, 0, 0)),
        ],
        out_shape=[
            jax.ShapeDtypeStruct((Bd, 1, B_V), BF16),
            jax.ShapeDtypeStruct((Bd, HB, DK, DV), F32),
        ],
        compiler_params=_cparams(("parallel",)),
        name="delta_sample",
    )(b_qkv.reshape(Bd, 1, CONV_DIM), conv_state, zg.reshape(Bd, 1, ZG_DIM), bg.reshape(Bd, 1, LANES),
      state, conv_w, head_params, norm_delta)
    return o_b.reshape(Bd, B_V), s_new


def _prepare(p):
    w_in = p['w_in']
    w_main = jnp.concatenate([w_in[:, :, :COL_BG], w_in[:, :, COL_GATE:]], axis=-1).astype(BF16)
    w_main = jnp.transpose(w_main.reshape(DEPTH, D_MODEL, N_MAIN // PROJ_TN, PROJ_TN), (0, 2, 1, 3))
    w_bg = jnp.pad(w_in[:, :, COL_BG:COL_GATE], ((0, 0), (0, 0), (0, LANES - 2 * HB))).astype(BF16)
    pad = lambda t: jnp.pad(t.astype(F32), ((0, 0), (HB, LANES - 2 * HB)))
    head_params = jnp.stack([pad(p['a_log']), pad(p['dt_bias'])], axis=1)
    vec = lambda t: t.astype(F32)[:, None, :]
    return dict(
        w_main=w_main, w_bg=w_bg, head_params=head_params,
        conv_w=p['conv_w'].astype(F32),
        wa=p['w_branch_a'].astype(BF16), wb=p['w_branch_b'].astype(BF16), wo=p['w_out'].astype(BF16),
        w_ffn_in=p['w_ffn_in'].astype(BF16), w_ffn_out=p['w_ffn_out'].astype(BF16),
        norm_pre_mix=vec(p['norm_pre_mix']), norm_post_mix=vec(p['norm_post_mix']),
        norm_pre_ffn=vec(p['norm_pre_ffn']), norm_post_ffn=vec(p['norm_post_ffn']),
        norm_delta=vec(p['norm_delta']),
    )


def _kv_rows(a_qkv, B, L, g, rows):
    t = a_qkv.reshape(B, L, N_GROUPS, 3, H_G, HD_A)
    return t[:, L - rows:, g, 1:]


def _trunk_prompt(x, w, rel_bias):
    B, L, _ = x.shape
    T = B * L
    tm = min(1024, T)
    x2d = x.reshape(T, D_MODEL)
    biases = [_prompt_bias(rel_bias, g) for g in range(N_GROUPS)]
    kv_out = [[] for _ in range(N_GROUPS)]
    s_out, conv_out = [], []
    for l in range(DEPTH):
        a0, a1, a2, b_qkv, zg, bg, kvt = _in_proj(x2d, w['norm_pre_mix'], w['w_main'], w['w_bg'], l, tm,
                                                  seq_len=L)
        o_a = _attn_prompt([a0, a1, a2], biases, B, L)
        o_b, s_fin = _delta_prompt(b_qkv, zg, bg, w['conv_w'], w['head_params'], w['norm_delta'], l, B, L)
        x2d = _mix(o_a, o_b, zg, x2d, w['wa'], w['wb'], w['wo'], w['norm_post_mix'], l, min(256, T))
        x2d = _ffn(x2d, w['norm_pre_ffn'], w['w_ffn_in'], w['w_ffn_out'], w['norm_post_ffn'], l, min(512, T))
        for g, (win, _) in enumerate(GROUPS):
            rows = min(win, L)
            kv_out[g].append(kvt[:, g, :, L - rows:].reshape(B, 2, H_G, HD_A, rows))
        s_out.append(s_fin)
        conv_out.append(b_qkv.reshape(B, L, CONV_DIM)[:, L - (CONV_W - 1):])
    kv = [jnp.transpose(jnp.stack(t), (0, 1, 5, 2, 3, 4)) for t in kv_out]
    return (x2d.reshape(B, L, D_MODEL), kv[0], kv[1], kv[2], jnp.stack(s_out), jnp.stack(conv_out))


def _trunk_sample(x, caches, state, conv_state, w, rel_bias):
    Bd = x.shape[0]
    x2d = x.reshape(Bd, D_MODEL)
    biases, bias_new = _sample_bias(rel_bias)
    lane_head = np.arange(A_OUT) // HD_A
    seg = jnp.asarray(lane_head[:, None] == lane_head[None, :], BF16)
    bb = min(8, Bd)
    for c, (win, dil) in zip(caches, GROUPS):
        assert c.shape[2] == N_CACHED * dil, "cache length must equal the group's window"
    caches_t = [jnp.transpose(c, (0, 1, 3, 4, 5, 2)) for c in caches]
    kv_out = [[] for _ in range(N_GROUPS)]
    s_out, conv_out = [], []
    for l in range(DEPTH):
        a_qkv, b_qkv, zg, bg = _in_proj(x2d, w['norm_pre_mix'], w['w_main'], w['w_bg'], l, Bd)
        o_a = _attn_sample(a_qkv, caches_t, biases, bias_new, seg, l)
        o_b, s_new = _delta_sample(b_qkv, conv_state, zg, bg, state, w['conv_w'], w['head_params'],
                                   w['norm_delta'], l, bb)
        x2d = _mix(o_a, o_b, zg, x2d, w['wa'], w['wb'], w['wo'], w['norm_post_mix'], l, Bd)
        x2d = _ffn(x2d, w['norm_pre_ffn'], w['w_ffn_in'], w['w_ffn_out'], w['norm_post_ffn'], l, Bd)
        for g in range(N_GROUPS):
            kv_out[g].append(_kv_rows(a_qkv, Bd, 1, g, 1))
        s_out.append(s_new)
        conv_out.append(jnp.concatenate([conv_state[l][:, 1:], b_qkv[:, None, :]], axis=1))
    return (x2d.reshape(Bd, 1, D_MODEL), jnp.stack(kv_out[0]), jnp.stack(kv_out[1]), jnp.stack(kv_out[2]),
            jnp.stack(s_out), jnp.stack(conv_out))


def kernel(x_prompt, x_sample, cache_kv_w128, cache_kv_w512, cache_kv_w2048, state_delta, state_conv,
           rel_bias, norm_pre_mix, w_in, conv_w, a_log, dt_bias, norm_delta, w_branch_a, w_branch_b,
           w_out, norm_post_mix, norm_pre_ffn, w_ffn_in, w_ffn_out, norm_post_ffn):
    p = dict(w_in=w_in, conv_w=conv_w, a_log=a_log, dt_bias=dt_bias, norm_delta=norm_delta,
             w_branch_a=w_branch_a, w_branch_b=w_branch_b, w_out=w_out, norm_pre_mix=norm_pre_mix,
             norm_post_mix=norm_post_mix, norm_pre_ffn=norm_pre_ffn, w_ffn_in=w_ffn_in,
             w_ffn_out=w_ffn_out, norm_post_ffn=norm_post_ffn)
    w = _prepare(p)
    assert x_sample.shape[1] == 1, "the decode trunk handles one new token per sequence"
    y_p, kv0_p, kv1_p, kv2_p, s_p, conv_p = _trunk_prompt(x_prompt, w, rel_bias)
    y_s, kv0_s, kv1_s, kv2_s, s_s, conv_s = _trunk_sample(
        x_sample, (cache_kv_w128, cache_kv_w512, cache_kv_w2048), state_delta, state_conv, w, rel_bias)
    return (y_p, y_s, kv0_p, kv1_p, kv2_p, s_p, conv_p, kv0_s, kv1_s, kv2_s, s_s, conv_s)
```

```python
import functools

import numpy as np
import jax
import jax.numpy as jnp
from jax import lax
from jax.experimental import pallas as pl
from jax.experimental.pallas import tpu as pltpu

F32 = jnp.float32
BF16 = jnp.bfloat16

D_MODEL = 1024
DEPTH = 4
GROUPS = ((128, 1), (512, 4), (2048, 16))
N_GROUPS = 3
HD_A = 64
H_G = 4
A_OUT = H_G * HD_A
A_GRP = 3 * A_OUT
A_QKV = N_GROUPS * A_GRP
N_KEYS = 129
Q_BLOCK = 128
ATTN_SCALE = HD_A ** -0.5
N_BUCKETS = 32
REL_MAX_DIST = 2048
HB = 8
DK = 128
DV = 128
B_QK = HB * DK
B_V = HB * DV
CONV_W = 4
CONV_DIM = 2 * B_QK + B_V
CHUNK = 64
D_FF = 2816
EPS = 1e-6
NEG = -1e30

COL_B = A_QKV
COL_Z = COL_B + CONV_DIM
COL_BG = COL_Z + B_V
COL_GATE = COL_BG + 2 * HB
N_IN = COL_GATE + 2 * D_MODEL
ZG_DIM = B_V + 2 * D_MODEL
N_MAIN = A_QKV + CONV_DIM + ZG_DIM
PROJ_TN = 768
N_TILES_A = A_QKV // PROJ_TN
N_TILES_B = CONV_DIM // PROJ_TN
N_TILES_ZG = ZG_DIM // PROJ_TN
LANES = 128
FFN_TF = D_FF // 2

VMEM_LIMIT = 52 * 1024 * 1024


VMEM_LIMIT_IN_PROJ = 58 * 1024 * 1024


def _cparams(sem, vmem_limit=VMEM_LIMIT):
    return pltpu.CompilerParams(dimension_semantics=sem, vmem_limit_bytes=vmem_limit)


def _sigmoid(x):
    return 1.0 / (1.0 + jnp.exp(-x))


def _silu(x):
    return x * _sigmoid(x)


def _softplus(x):
    return jnp.maximum(x, 0.0) + jnp.log(1.0 + jnp.exp(-jnp.abs(x)))


def _rms(x, w):
    return x * lax.rsqrt(jnp.mean(x * x, axis=-1, keepdims=True) + EPS) * w


def _dot(a, b):
    return jnp.dot(a.astype(BF16), b.astype(BF16), preferred_element_type=F32)


def _dot_nt(a, b):
    return lax.dot_general(a.astype(BF16), b.astype(BF16), (((1,), (1,)), ((), ())),
                           preferred_element_type=F32)


def _dot_f32(a, b):
    return jnp.dot(a, b, preferred_element_type=F32, precision=lax.Precision.HIGHEST)


def _in_proj_kernel(x_ref, nw_ref, w_ref, wbg_ref, *refs, prompt):
    if prompt:
        oa_refs, (ob_ref, ozg_ref, obg_ref, kvt_ref, h_ref, slab_ref) = refs[:N_GROUPS], refs[N_GROUPS:]
    else:
        oa_ref, ob_ref, ozg_ref, obg_ref, h_ref = refs
    j = pl.program_id(1)
    tm = x_ref.shape[0]
    n_slabs = PROJ_TN // LANES
    tile = (lambda: w_ref[j]) if len(w_ref.shape) == 3 else (lambda: w_ref[...])

    @pl.when(j == 0)
    def _():
        hb = _rms(x_ref[...], nw_ref[...]).astype(BF16)
        h_ref[...] = hb
        obg_ref[...] = jnp.dot(hb, wbg_ref[...], preferred_element_type=F32)

    def group_tile(g):
        acc = jnp.dot(h_ref[...], tile(), preferred_element_type=F32)
        if not prompt:
            oa_ref[...] = acc
            return
        kvt_ref[...] = acc[:, A_OUT:].T
        dil = GROUPS[g][1]
        qkv = jnp.concatenate([acc[:, :A_OUT] * ATTN_SCALE, acc[:, A_OUT:]], axis=1)
        if dil == 1:
            oa_refs[g][0] = qkv.astype(BF16)
            return
        for c in range(n_slabs):
            slab_ref[c] = qkv[:, c * LANES:(c + 1) * LANES]
        for r in range(dil):
            for c in range(n_slabs):
                rows = slab_ref[c, pl.ds(r, tm // dil, stride=dil), :]
                oa_refs[g][r, :, c * LANES:(c + 1) * LANES] = rows.astype(BF16)

    for g in range(N_TILES_A):
        pl.when(j == g)(functools.partial(group_tile, g))

    @pl.when((j >= N_TILES_A) & (j < N_TILES_A + N_TILES_B))
    def _():
        ob_ref[...] = jnp.dot(h_ref[...], tile(), preferred_element_type=F32)

    @pl.when(j >= N_TILES_A + N_TILES_B)
    def _():
        ozg_ref[...] = jnp.dot(h_ref[...], tile(), preferred_element_type=F32).astype(ozg_ref.dtype)


def _in_proj(x2d, norm_w, w_main, w_bg, l, tm, seq_len=None):
    assert A_GRP == PROJ_TN, "one column tile per attention group"
    T = x2d.shape[0]
    n_tiles = N_TILES_A + N_TILES_B + N_TILES_ZG
    last_a, first_zg = N_TILES_A - 1, N_TILES_A + N_TILES_B
    prompt = seq_len is not None
    scratch = [pltpu.VMEM((tm, D_MODEL), BF16)]
    if prompt:
        per_seq = seq_len // tm
        n_seq = T // seq_len
        a_specs = [pl.BlockSpec((None, dil, tm // dil, A_GRP), lambda i, j: (i // per_seq, 0, i % per_seq, 0))
                   for _, dil in GROUPS]
        a_shapes = [jax.ShapeDtypeStruct((n_seq, dil, seq_len // dil, A_GRP), BF16) for _, dil in GROUPS]
        extra_specs = [pl.BlockSpec((None, None, 2 * A_OUT, tm),
                                    lambda i, j: (i // per_seq, jnp.minimum(j, last_a), 0, i % per_seq))]
        extra_shapes = [jax.ShapeDtypeStruct((n_seq, N_GROUPS, 2 * A_OUT, seq_len), F32)]
        scratch.append(pltpu.VMEM((PROJ_TN // LANES, tm, LANES), F32))
        w_spec = pl.BlockSpec((None, n_tiles, D_MODEL, PROJ_TN), lambda i, j: (l, 0, 0, 0),
                              pipeline_mode=pl.Buffered(1))
        vmem_limit = VMEM_LIMIT_IN_PROJ
    else:
        a_specs = [pl.BlockSpec((tm, PROJ_TN), lambda i, j: (i, jnp.minimum(j, last_a)))]
        a_shapes = [jax.ShapeDtypeStruct((T, A_QKV), F32)]
        extra_specs, extra_shapes = [], []
        w_spec = pl.BlockSpec((None, None, D_MODEL, PROJ_TN), lambda i, j: (l, j, 0, 0))
        vmem_limit = VMEM_LIMIT
    return pl.pallas_call(
        functools.partial(_in_proj_kernel, prompt=prompt),
        grid=(T // tm, n_tiles),
        in_specs=[
            pl.BlockSpec((tm, D_MODEL), lambda i, j: (i, 0)),
            pl.BlockSpec((None, 1, D_MODEL), lambda i, j: (l, 0, 0)),
            w_spec,
            pl.BlockSpec((None, D_MODEL, LANES), lambda i, j: (l, 0, 0)),
        ],
        out_specs=a_specs + [
            pl.BlockSpec((tm, PROJ_TN), lambda i, j: (i, jnp.clip(j - N_TILES_A, 0, N_TILES_B - 1))),
            pl.BlockSpec((tm, PROJ_TN), lambda i, j: (i, jnp.maximum(j - first_zg, 0))),
            pl.BlockSpec((tm, LANES), lambda i, j: (i, 0)),
        ] + extra_specs,
        out_shape=a_shapes + [
            jax.ShapeDtypeStruct((T, CONV_DIM), F32),
            jax.ShapeDtypeStruct((T, ZG_DIM), BF16),
            jax.ShapeDtypeStruct((T, LANES), F32),
        ] + extra_shapes,
        scratch_shapes=scratch,
        compiler_params=_cparams(("parallel", "arbitrary"), vmem_limit),
        name="in_proj",
    )(x2d, norm_w, w_main, w_bg)


def _t5_bucket(dist):
    max_exact = N_BUCKETS // 2
    d = np.maximum(dist, 1).astype(np.float32)
    large = max_exact + (np.log(d / max_exact) / np.log(REL_MAX_DIST / max_exact)
                         * (N_BUCKETS - max_exact)).astype(np.int32)
    large = np.minimum(large, N_BUCKETS - 1)
    return np.where(dist < max_exact, dist, large).astype(np.int32)


def _group_bias(rel_bias, g):
    dil = GROUPS[g][1]
    bucket = _t5_bucket(np.arange(N_KEYS) * dil)
    return rel_bias[bucket][:, g * H_G:(g + 1) * H_G].astype(F32)


def _prompt_bias(rel_bias, g):
    period = 3 * Q_BLOCK
    vals = _group_bias(rel_bias, g)
    u = jnp.concatenate([vals[::-1], jnp.full((period - N_KEYS, H_G), NEG, F32)], axis=0).T
    flat = jnp.tile(u, (1, Q_BLOCK))[:, :Q_BLOCK * (period - 1)]
    return flat.reshape(H_G, Q_BLOCK, period - 1)[:, :, :2 * Q_BLOCK]


def _attn_group(g, a_ref, bias_ref, emit):
    dil, lc, _ = a_ref.shape
    nb = lc // Q_BLOCK
    lane_head = lax.broadcasted_iota(jnp.int32, (1, A_OUT), 1) // HD_A
    hms = [lane_head == h for h in range(H_G)]
    blocks = [(r, pb) for r in range(dil) for pb in range(nb)]
    for i in range(0, len(blocks), 2):
        items = blocks[i:i + 2]
        work = [(it, h) for it in items for h in range(H_G)]
        keys = lambda r, pb, part: a_ref[r, pl.ds((pb - 1) * Q_BLOCK, 2 * Q_BLOCK) if pb > 0
                                         else pl.ds(0, Q_BLOCK), part * A_OUT:(part + 1) * A_OUT]
        s = {}
        for (r, pb), h in work:
            q = a_ref[r, pl.ds(pb * Q_BLOCK, Q_BLOCK), 0:A_OUT]
            qh = jnp.where(hms[h], q, jnp.zeros_like(q))
            b = bias_ref[h] if pb > 0 else bias_ref[h, :, Q_BLOCK:2 * Q_BLOCK]
            s[r, pb, h] = _dot_nt(qh, keys(r, pb, 1)) + b
        pn, lse = {}, {}
        for (r, pb), h in work:
            m = jnp.max(s[r, pb, h], axis=-1, keepdims=True)
            p = jnp.exp(s[r, pb, h] - m)
            den = jnp.sum(p, axis=-1, keepdims=True)
            pn[r, pb, h] = (p / den).astype(BF16)
            lse[r, pb, h] = m + jnp.log(den)
        oh = {}
        for (r, pb), h in work:
            oh[r, pb, h] = jnp.dot(pn[r, pb, h], keys(r, pb, 2), preferred_element_type=F32)
        for r, pb in items:
            o_acc, l_acc = oh[r, pb, 0], lse[r, pb, 0]
            for h in range(1, H_G):
                o_acc = jnp.where(hms[h], oh[r, pb, h], o_acc)
                l_acc = jnp.where(hms[h], lse[r, pb, h], l_acc)
            emit(r, pb, o_acc, jnp.broadcast_to(l_acc, (Q_BLOCK, A_OUT)))


def _attn_prompt_kernel(a0_ref, a1_ref, a2_ref, b0_ref, b1_ref, b2_ref, o_ref,
                        od_ref, ld_ref, orun_ref, lrun_ref):
    L = o_ref.shape[0]
    halves = A_OUT // LANES
    for g, (a_ref, bias_ref) in enumerate(((a0_ref, b0_ref), (a1_ref, b1_ref), (a2_ref, b2_ref))):
        dil = a_ref.shape[0]
        lc = L // dil
        if dil == 1:
            assert g == 0

            def emit(r, pb, o, lse):
                for c in range(halves):
                    orun_ref[c, pl.ds(pb * Q_BLOCK, Q_BLOCK), :] = o[:, c * LANES:(c + 1) * LANES]
                    lrun_ref[c, pl.ds(pb * Q_BLOCK, Q_BLOCK), :] = lse[:, c * LANES:(c + 1) * LANES]

            _attn_group(g, a_ref, bias_ref, emit)
            continue

        def emit(r, pb, o, lse, lc=lc):
            od_ref[pl.ds(r * lc + pb * Q_BLOCK, Q_BLOCK), :] = o
            ld_ref[pl.ds(r * lc + pb * Q_BLOCK, Q_BLOCK), :] = lse

        _attn_group(g, a_ref, bias_ref, emit)
        for r in range(dil):
            rows = pl.ds(r, lc, stride=dil)
            src = pl.ds(r * lc, lc)
            for c in range(halves):
                cols = pl.ds(c * LANES, LANES)
                o_new, l_new = od_ref[src, cols], ld_ref[src, cols]
                o_old, l_old = orun_ref[c, rows, :], lrun_ref[c, rows, :]
                m = jnp.maximum(l_old, l_new)
                e_old, e_new = jnp.exp(l_old - m), jnp.exp(l_new - m)
                den = e_old + e_new
                orun_ref[c, rows, :] = (e_old * o_old + e_new * o_new) / den
                if g < N_GROUPS - 1:
                    lrun_ref[c, rows, :] = m + jnp.log(den)
    for c in range(halves):
        o_ref[:, c * LANES:(c + 1) * LANES] = orun_ref[c]


def _attn_prompt(a_groups, biases, B, L):
    halves = A_OUT // LANES
    a_specs = [pl.BlockSpec((None,) + a.shape[1:], lambda b: (b, 0, 0, 0)) for a in a_groups]
    bias_spec = pl.BlockSpec((H_G, Q_BLOCK, 2 * Q_BLOCK), lambda b: (0, 0, 0))
    return pl.pallas_call(
        _attn_prompt_kernel,
        grid=(B,),
        in_specs=a_specs + [bias_spec] * N_GROUPS,
        out_specs=pl.BlockSpec((L, A_OUT), lambda b: (b, 0)),
        out_shape=jax.ShapeDtypeStruct((B * L, A_OUT), F32),
        scratch_shapes=[pltpu.VMEM((L, A_OUT), F32), pltpu.VMEM((L, A_OUT), F32),
                        pltpu.VMEM((halves, L, LANES), F32), pltpu.VMEM((halves, L, LANES), F32)],
        compiler_params=_cparams(("parallel",)),
        name="attn_prompt",
    )(*a_groups, *biases)


def _delta_prompt_kernel(bq_ref, bk_ref, bv_ref, z_ref, bg_ref, cw_ref, hp_ref, nd_ref,
                         ob_ref, sout_ref, s_ref, xc_ref):
    c = pl.program_id(1)
    C = CHUNK
    n_seq = bq_ref.shape[0]

    @pl.when(c == 0)
    def _():
        s_ref[...] = jnp.zeros_like(s_ref)
        xc_ref[:, 0:8, :] = jnp.zeros((n_seq, 8, CONV_DIM), F32)

    xc_ref[:, 8:8 + C, 0:B_QK] = bq_ref[...]
    xc_ref[:, 8:8 + C, B_QK:2 * B_QK] = bk_ref[...]
    xc_ref[:, 8:8 + C, 2 * B_QK:CONV_DIM] = bv_ref[...]

    def conv(i, lo):
        y = cw_ref[0:1, lo:lo + DK] * xc_ref[i, 5:5 + C, lo:lo + DK]
        for j in range(1, CONV_W):
            y = y + cw_ref[j:j + 1, lo:lo + DK] * xc_ref[i, 5 + j:5 + j + C, lo:lo + DK]
        return _silu(y)

    hp = hp_ref[...]
    row = lax.broadcasted_iota(jnp.int32, (C, C), 0)
    colc = lax.broadcasted_iota(jnp.int32, (C, C), 1)
    tril = row >= colc
    tril_f = tril.astype(F32)
    eye = (row == colc).astype(F32)
    r2 = lax.broadcasted_iota(jnp.int32, (LANES, LANES), 0)
    c2 = lax.broadcasted_iota(jnp.int32, (LANES, LANES), 1)
    eye_l = (r2 == c2).astype(F32)
    nd = nd_ref[...]

    units = [(i, h) for i in range(n_seq) for h in range(HB)]
    idx = range(len(units))
    ks, kqs, vbs, kbgs, qgs, kdts, decays, elasts = [], [], [], [], [], [], [], []
    for i in range(n_seq):
        bg = bg_ref[i]
        beta_all = _sigmoid(bg)
        g_all = -jnp.exp(hp[0:1, :]) * _softplus(bg + hp[1:2, :])
        gc = _dot_f32(tril_f, g_all)
        gct = lax.dot_general(eye_l, gc, (((1,), (1,)), ((), ())), preferred_element_type=F32,
                              precision=lax.Precision.HIGHEST)
        for h in range(HB):
            lo = h * DK
            q = conv(i, lo)
            k = conv(i, B_QK + lo)
            v = conv(i, 2 * B_QK + lo)
            q = q * (lax.rsqrt(jnp.sum(q * q, axis=-1, keepdims=True) + EPS) * (DK ** -0.5))
            k = k * lax.rsqrt(jnp.sum(k * k, axis=-1, keepdims=True) + EPS)
            beta = beta_all[:, h:h + 1]
            gcol = gc[:, HB + h:HB + h + 1]
            grow = gct[HB + h:HB + h + 1, :]
            glast = gc[C - 1:C, HB + h:HB + h + 1]
            egc = jnp.exp(gcol)
            kb = k * beta
            ks.append(k.astype(BF16))
            kqs.append(jnp.concatenate([kb, q], axis=0).astype(BF16))
            vbs.append(v * beta)
            kbgs.append(kb * egc)
            qgs.append(q * egc)
            kdts.append((k * jnp.exp(glast - gcol)).T)
            decays.append(jnp.exp(jnp.where(tril, gcol - grow, NEG)))
            elasts.append(jnp.exp(glast))
    kq = [_dot_nt(kqs[u], ks[u]) for u in idx]
    pw = [jnp.where(row > colc, kq[u][0:C] * decays[u], 0.0) for u in idx]
    attn = [kq[u][C:2 * C] * decays[u] for u in idx]
    p = [eye - pw[u] for u in idx]
    pw = [_dot(pw[u], pw[u]) for u in idx]
    n = 4
    while n < C:
        nxt = [_dot(pw[u], pw[u]) for u in idx]
        p = [p[u] + _dot(p[u], pw[u]) for u in idx]
        pw = nxt
        n *= 2
    p = [p[u] + _dot(p[u], pw[u]) for u in idx]
    uw = [_dot(p[u], jnp.concatenate([vbs[u], kbgs[u]], axis=1)) for u in idx]
    s_old = [s_ref[i, h] for i, h in units]
    ws_qs = [_dot(jnp.concatenate([uw[u][:, DV:DV + DK], qgs[u]], axis=0), s_old[u]) for u in idx]
    v_new = [uw[u][:, 0:DV] - ws_qs[u][0:C] for u in idx]
    av = [_dot(jnp.concatenate([attn[u], kdts[u]], axis=0), v_new[u]) for u in idx]
    for u, (i, h) in enumerate(units):
        lo = h * DK
        s_ref[i, h] = s_old[u] * elasts[u] + av[u][C:C + DK]
        o = ws_qs[u][C:2 * C] + av[u][0:C]
        o = o * lax.rsqrt(jnp.mean(o * o, axis=-1, keepdims=True) + EPS) * nd
        ob_ref[i, :, lo:lo + DV] = (o * _silu(z_ref[i, :, lo:lo + DV].astype(F32))).astype(ob_ref.dtype)

    xc_ref[:, 0:8, :] = xc_ref[:, C:C + 8, :]

    @pl.when(c == pl.num_programs(1) - 1)
    def _():
        sout_ref[...] = s_ref[...]


DELTA_SEQS = 4


def _delta_prompt(b_qkv, zg, bg, conv_w, head_params, norm_delta, l, B, L):
    n = L // CHUNK
    ns = DELTA_SEQS if B % DELTA_SEQS == 0 else 1
    seq = lambda t: t.reshape(B, L, t.shape[-1])
    rows = lambda w, part: pl.BlockSpec((ns, CHUNK, w), lambda b, c: (b, c, part))
    o_b, s_fin = pl.pallas_call(
        _delta_prompt_kernel,
        grid=(B // ns, n),
        in_specs=[
            rows(B_QK, 0), rows(B_QK, 1), rows(B_QK, 2),
            rows(B_V, 0),
            rows(LANES, 0),
            pl.BlockSpec((None, CONV_W, CONV_DIM), lambda b, c: (l, 0, 0)),
            pl.BlockSpec((None, 2, LANES), lambda b, c: (l, 0, 0)),
            pl.BlockSpec((None, 1, DV), lambda b, c: (l, 0, 0)),
        ],
        out_specs=[
            rows(B_V, 0),
            pl.BlockSpec((ns, HB, DK, DV), lambda b, c: (b, 0, 0, 0)),
        ],
        out_shape=[
            jax.ShapeDtypeStruct((B, L, B_V), BF16),
            jax.ShapeDtypeStruct((B, HB, DK, DV), F32),
        ],
        scratch_shapes=[pltpu.VMEM((ns, HB, DK, DV), F32), pltpu.VMEM((ns, 8 + CHUNK, CONV_DIM), F32)],
        compiler_params=_cparams(("parallel", "arbitrary")),
        name="delta_prompt",
    )(seq(b_qkv), seq(b_qkv), seq(b_qkv), seq(zg), seq(bg), conv_w, head_params, norm_delta)
    return o_b.reshape(B * L, B_V), s_fin


def _merge_groups(os_, lses):
    m = jnp.maximum(jnp.maximum(lses[0], lses[1]), lses[2])
    es = [jnp.exp(t - m) for t in lses]
    den = es[0] + es[1] + es[2]
    return (es[0] * os_[0] + es[1] * os_[1] + es[2] * os_[2]) / den


def _mix_kernel(oa_ref, ob_ref, ga_ref, gb_ref, x_ref, wa_ref, wb_ref, wo_ref, nw_ref, out_ref):
    ya = _dot(oa_ref[...], wa_ref[...])
    yb = jnp.dot(ob_ref[...], wb_ref[...], preferred_element_type=F32)
    mix = _sigmoid(ga_ref[...].astype(F32)) * ya + _sigmoid(gb_ref[...].astype(F32)) * yb
    y = _dot(mix, wo_ref[...])
    out_ref[...] = x_ref[...] + _rms(y, nw_ref[...])


def _mix(o_a, o_b, zg, x2d, wa, wb, wo, norm_w, l, tm):
    T = x2d.shape[0]
    rows = lambda w: pl.BlockSpec((tm, w), lambda i: (i, 0))
    return pl.pallas_call(
        _mix_kernel,
        grid=(T // tm,),
        in_specs=[
            rows(A_OUT),
            rows(B_V),
            pl.BlockSpec((tm, D_MODEL), lambda i: (i, 1)),
            pl.BlockSpec((tm, D_MODEL), lambda i: (i, 2)),
            rows(D_MODEL),
            pl.BlockSpec((None, A_OUT, D_MODEL), lambda i: (l, 0, 0)),
            pl.BlockSpec((None, B_V, D_MODEL), lambda i: (l, 0, 0)),
            pl.BlockSpec((None, D_MODEL, D_MODEL), lambda i: (l, 0, 0)),
            pl.BlockSpec((None, 1, D_MODEL), lambda i: (l, 0, 0)),
        ],
        out_specs=rows(D_MODEL),
        out_shape=jax.ShapeDtypeStruct((T, D_MODEL), F32),
        compiler_params=_cparams(("parallel",)),
        name="mix",
    )(o_a, o_b, zg, zg, x2d, wa, wb, wo, norm_w)


def _ffn_kernel(x_ref, nw1_ref, wg_ref, wu_ref, wd_ref, nw2_ref, out_ref, h_ref, acc_ref):
    j = pl.program_id(1)

    @pl.when(j == 0)
    def _():
        h_ref[...] = _rms(x_ref[...], nw1_ref[...]).astype(BF16)
        acc_ref[...] = jnp.zeros_like(acc_ref)

    h = h_ref[...]
    gt = jnp.dot(h, wg_ref[...], preferred_element_type=F32)
    up = jnp.dot(h, wu_ref[...], preferred_element_type=F32)
    acc_ref[...] += _dot(_silu(gt) * up, wd_ref[...])

    @pl.when(j == pl.num_programs(1) - 1)
    def _():
        out_ref[...] = x_ref[...] + _rms(acc_ref[...], nw2_ref[...])


def _ffn(x2d, nw1, w_in, w_out, nw2, l, tm):
    T = x2d.shape[0]
    nf = D_FF // FFN_TF
    return pl.pallas_call(
        _ffn_kernel,
        grid=(T // tm, nf),
        in_specs=[
            pl.BlockSpec((tm, D_MODEL), lambda i, j: (i, 0)),
            pl.BlockSpec((None, 1, D_MODEL), lambda i, j: (l, 0, 0)),
            pl.BlockSpec((None, D_MODEL, FFN_TF), lambda i, j: (l, 0, j)),
            pl.BlockSpec((None, D_MODEL, FFN_TF), lambda i, j: (l, 0, nf + j)),
            pl.BlockSpec((None, FFN_TF, D_MODEL), lambda i, j: (l, j, 0)),
            pl.BlockSpec((None, 1, D_MODEL), lambda i, j: (l, 0, 0)),
        ],
        out_specs=pl.BlockSpec((tm, D_MODEL), lambda i, j: (i, 0)),
        out_shape=jax.ShapeDtypeStruct((T, D_MODEL), F32),
        scratch_shapes=[pltpu.VMEM((tm, D_MODEL), BF16), pltpu.VMEM((tm, D_MODEL), F32)],
        compiler_params=_cparams(("parallel", "arbitrary")),
        name="ffn",
    )(x2d, nw1, w_in, w_in, w_out, nw2)


N_CACHED = N_KEYS - 1


def _attn_sample_kernel(a_ref, c0_ref, c1_ref, c2_ref, b0_ref, b1_ref, b2_ref, bnew_ref, seg_ref, o_ref):
    caches = (c0_ref, c1_ref, c2_ref)
    biases = (b0_ref, b1_ref, b2_ref)
    seg = seg_ref[...]
    a = a_ref[0]
    gh = [(g, h) for g in range(N_GROUPS) for h in range(H_G)]
    rows8 = lambda t: jnp.broadcast_to(t, (8, t.shape[-1]))
    q, vn, s_new = [], [], []
    for g in range(N_GROUPS):
        lo = g * A_GRP
        q.append(a[:, lo:lo + A_OUT] * ATTN_SCALE)
        vn.append(a[:, lo + 2 * A_OUT:lo + 3 * A_OUT])
        s_new.append(_dot(rows8(q[g] * a[:, lo + A_OUT:lo + 2 * A_OUT]), seg) + bnew_ref[g])
    head = lambda t, h: t[:, h * HD_A:(h + 1) * HD_A]
    s = {(g, h): _dot(rows8(head(q[g], h)), caches[g][0, h]) + biases[g][h:h + 1, :] for g, h in gh}
    pn, p_new, lse = {}, {}, {}
    for g, h in gh:
        sn = s_new[g][:, h * HD_A:h * HD_A + 1]
        m = jnp.maximum(jnp.max(s[g, h], axis=-1, keepdims=True), sn)
        p = jnp.exp(s[g, h] - m)
        e_new = jnp.exp(sn - m)
        den = jnp.sum(p, axis=-1, keepdims=True) + e_new
        pn[g, h] = p / den
        p_new[g, h] = e_new / den
        lse[g, h] = m + jnp.log(den)
    o = {(g, h): _dot_nt(pn[g, h], caches[g][1, h]) + p_new[g, h] * head(vn[g], h) for g, h in gh}
    merged = [_merge_groups([o[g, h] for g in range(N_GROUPS)], [lse[g, h] for g in range(N_GROUPS)])
              for h in range(H_G)]
    o_ref[0] = jnp.concatenate(merged, axis=1)[0:1, :]


def _attn_sample(a_qkv, caches_t, biases, bias_new, seg, l):
    Bd = a_qkv.shape[0]
    cache_specs = [pl.BlockSpec((None, None) + c.shape[2:], lambda i: (l, i, 0, 0, 0, 0)) for c in caches_t]
    bias_specs = [pl.BlockSpec(b.shape, lambda i: (0, 0)) for b in biases]
    o = pl.pallas_call(
        _attn_sample_kernel,
        grid=(Bd,),
        in_specs=[pl.BlockSpec((1, 1, A_QKV), lambda i: (i, 0, 0))] + cache_specs + bias_specs + [
            pl.BlockSpec((N_GROUPS, 1, A_OUT), lambda i: (0, 0, 0)),
            pl.BlockSpec((A_OUT, A_OUT), lambda i: (0, 0)),
        ],
        out_specs=pl.BlockSpec((1, 1, A_OUT), lambda i: (i, 0, 0)),
        out_shape=jax.ShapeDtypeStruct((Bd, 1, A_OUT), F32),
        compiler_params=_cparams(("parallel",)),
        name="attn_sample",
    )(a_qkv.reshape(Bd, 1, A_QKV), *caches_t, *biases, bias_new, seg)
    return o.reshape(Bd, A_OUT)


def _sample_bias(rel_bias):
    rows, new = [], []
    for g, (win, dil) in enumerate(GROUPS):
        vals = _group_bias(rel_bias, g)
        sel = vals[1:][::-1][:, None, :]
        skip = jnp.full((N_CACHED, dil - 1, H_G), NEG, F32)
        rows.append(jnp.concatenate([sel, skip], axis=1).reshape(N_CACHED * dil, H_G).T)
        new.append(jnp.repeat(vals[0:1], HD_A, axis=1))
    return rows, jnp.stack(new)


def _delta_sample_kernel(bn_ref, cb_ref, z_ref, bg_ref, s_ref, cw_ref, hp_ref, nd_ref, ob_ref, sout_ref):
    bb = bn_ref.shape[0]
    bg = bg_ref[...]
    hp = hp_ref[...]
    beta_all = _sigmoid(bg)
    g_all = -jnp.exp(hp[0:1, :]) * _softplus(bg + hp[1:2, :])
    eg_all = jnp.exp(g_all)
    nd = nd_ref[...]
    r8 = lax.broadcasted_iota(jnp.int32, (1, 8, 1), 1)
    r2 = lax.broadcasted_iota(jnp.int32, (DK, DK), 0)
    c2 = lax.broadcasted_iota(jnp.int32, (DK, DK), 1)
    eye = jnp.broadcast_to((r2 == c2).astype(BF16)[None], (bb, DK, DK))

    def conv(lo):
        y = cw_ref[CONV_W - 1:CONV_W, lo:lo + DK] * bn_ref[:, :, lo:lo + DK]
        for j in range(CONV_W - 1):
            y = y + cw_ref[j:j + 1, lo:lo + DK] * cb_ref[:, j:j + 1, lo:lo + DK]
        return _silu(y)

    for h in range(HB):
        lo = h * DK
        q = conv(lo)
        k = conv(B_QK + lo)
        v = conv(2 * B_QK + lo)
        q = q * lax.rsqrt(jnp.sum(q * q, axis=-1, keepdims=True) + EPS) * (DK ** -0.5)
        k = k * lax.rsqrt(jnp.sum(k * k, axis=-1, keepdims=True) + EPS)
        beta = beta_all[:, :, h:h + 1]
        eg = eg_all[:, :, HB + h:HB + h + 1]
        s = s_ref[:, h]
        w = k * (beta * eg)
        qg = q * eg
        lhs = jnp.where(r8 == 0, w, jnp.where(r8 == 1, qg, 0.0))
        r = jnp.einsum('bmk,bkv->bmv', lhs.astype(BF16), s.astype(BF16), preferred_element_type=F32)
        v_new = v * beta - r[:, 0:1, :]
        attn = jnp.sum(q * k, axis=-1, keepdims=True)
        o = r[:, 1:2, :] + attn * v_new
        kt = jnp.einsum('bij,bmj->bim', eye, jnp.broadcast_to(k, (bb, 8, DK)).astype(BF16),
                        preferred_element_type=F32)
        sout_ref[:, h] = s * eg + kt[:, :, 0:1] * v_new
        o = o * lax.rsqrt(jnp.mean(o * o, axis=-1, keepdims=True) + EPS) * nd
        ob_ref[:, :, lo:lo + DV] = (o * _silu(z_ref[:, :, lo:lo + DV].astype(F32))).astype(ob_ref.dtype)


def _delta_sample(b_qkv, conv_state, zg, bg, state, conv_w, head_params, norm_delta, l, bb):
    Bd = b_qkv.shape[0]
    row = lambda w: pl.BlockSpec((bb, 1, w), lambda i: (i, 0, 0))
    o_b, s_new = pl.pallas_call(
        _delta_sample_kernel,
        grid=(Bd // bb,),
        in_specs=[
            row(CONV_DIM),
            pl.BlockSpec((None, bb, CONV_W - 1, CONV_DIM), lambda i: (l, i, 0, 0)),
            row(B_V),
            row(LANES),
            pl.BlockSpec((None, bb, HB, DK, DV), lambda i: (l, i, 0, 0, 0)),
            pl.BlockSpec((None, CONV_W, CONV_DIM), lambda i: (l, 0, 0)),
            pl.BlockSpec((None, 2, LANES), lambda i: (l, 0, 0)),
            pl.BlockSpec((None, 1, DV), lambda i: (l, 0, 0)),
        ],
        out_specs=[
            row(B_V),
            pl.BlockSpec((bb, HB, DK, DV), lambda i: (i, 0, 0, 0)),
        ],
        out_shape=[
            jax.ShapeDtypeStruct((Bd, 1, B_V), BF16),
            jax.ShapeDtypeStruct((Bd, HB, DK, DV), F32),
        ],
        compiler_params=_cparams(("parallel",)),
        name="delta_sample",
    )(b_qkv.reshape(Bd, 1, CONV_DIM), conv_state, zg.reshape(Bd, 1, ZG_DIM), bg.reshape(Bd, 1, LANES),
      state, conv_w, head_params, norm_delta)
    return o_b.reshape(Bd, B_V), s_new


def _prepare(p):
    w_in = p['w_in']
    w_main = jnp.concatenate([w_in[:, :, :COL_BG], w_in[:, :, COL_GATE:]], axis=-1).astype(BF16)
    w_main = jnp.transpose(w_main.reshape(DEPTH, D_MODEL, N_MAIN // PROJ_TN, PROJ_TN), (0, 2, 1, 3))
    w_bg = jnp.pad(w_in[:, :, COL_BG:COL_GATE], ((0, 0), (0, 0), (0, LANES - 2 * HB))).astype(BF16)
    pad = lambda t: jnp.pad(t.astype(F32), ((0, 0), (HB, LANES - 2 * HB)))
    head_params = jnp.stack([pad(p['a_log']), pad(p['dt_bias'])], axis=1)
    vec = lambda t: t.astype(F32)[:, None, :]
    return dict(
        w_main=w_main, w_bg=w_bg, head_params=head_params,
        conv_w=p['conv_w'].astype(F32),
        wa=p['w_branch_a'].astype(BF16), wb=p['w_branch_b'].astype(BF16), wo=p['w_out'].astype(BF16),
        w_ffn_in=p['w_ffn_in'].astype(BF16), w_ffn_out=p['w_ffn_out'].astype(BF16),
        norm_pre_mix=vec(p['norm_pre_mix']), norm_post_mix=vec(p['norm_post_mix']),
        norm_pre_ffn=vec(p['norm_pre_ffn']), norm_post_ffn=vec(p['norm_post_ffn']),
        norm_delta=vec(p['norm_delta']),
    )


def _kv_rows(a_qkv, B, L, g, rows):
    t = a_qkv.reshape(B, L, N_GROUPS, 3, H_G, HD_A)
    return t[:, L - rows:, g, 1:]


def _trunk_prompt(x, w, rel_bias):
    B, L, _ = x.shape
    T = B * L
    tm = min(1024, T)
    x2d = x.reshape(T, D_MODEL)
    biases = [_prompt_bias(rel_bias, g) for g in range(N_GROUPS)]
    kv_out = [[] for _ in range(N_GROUPS)]
    s_out, conv_out = [], []
    for l in range(DEPTH):
        a0, a1, a2, b_qkv, zg, bg, kvt = _in_proj(x2d, w['norm_pre_mix'], w['w_main'], w['w_bg'], l, tm,
                                                  seq_len=L)
        o_a = _attn_prompt([a0, a1, a2], biases, B, L)
        o_b, s_fin = _delta_prompt(b_qkv, zg, bg, w['conv_w'], w['head_params'], w['norm_delta'], l, B, L)
        x2d = _mix(o_a, o_b, zg, x2d, w['wa'], w['wb'], w['wo'], w['norm_post_mix'], l, min(256, T))
        x2d = _ffn(x2d, w['norm_pre_ffn'], w['w_ffn_in'], w['w_ffn_out'], w['norm_post_ffn'], l, min(512, T))
        for g, (win, _) in enumerate(GROUPS):
            rows = min(win, L)
            kv_out[g].append(kvt[:, g, :, L - rows:].reshape(B, 2, H_G, HD_A, rows))
        s_out.append(s_fin)
        conv_out.append(b_qkv.reshape(B, L, CONV_DIM)[:, L - (CONV_W - 1):])
    kv = [jnp.transpose(jnp.stack(t), (0, 1, 5, 2, 3, 4)) for t in kv_out]
    return (x2d.reshape(B, L, D_MODEL), kv[0], kv[1], kv[2], jnp.stack(s_out), jnp.stack(conv_out))


def _trunk_sample(x, caches, state, conv_state, w, rel_bias):
    Bd = x.shape[0]
    x2d = x.reshape(Bd, D_MODEL)
    biases, bias_new = _sample_bias(rel_bias)
    lane_head = np.arange(A_OUT) // HD_A
    seg = jnp.asarray(lane_head[:, None] == lane_head[None, :], BF16)
    bb = min(8, Bd)
    for c, (win, dil) in zip(caches, GROUPS):
        assert c.shape[2] == N_CACHED * dil, "cache length must equal the group's window"
    caches_t = [jnp.transpose(c, (0, 1, 3, 4, 5, 2)) for c in caches]
    kv_out = [[] for _ in range(N_GROUPS)]
    s_out, conv_out = [], []
    for l in range(DEPTH):
        a_qkv, b_qkv, zg, bg = _in_proj(x2d, w['norm_pre_mix'], w['w_main'], w['w_bg'], l, Bd)
        o_a = _attn_sample(a_qkv, caches_t, biases, bias_new, seg, l)
        o_b, s_new = _delta_sample(b_qkv, conv_state, zg, bg, state, w['conv_w'], w['head_params'],
                                   w['norm_delta'], l, bb)
        x2d = _mix(o_a, o_b, zg, x2d, w['wa'], w['wb'], w['wo'], w['norm_post_mix'], l, Bd)
        x2d = _ffn(x2d, w['norm_pre_ffn'], w['w_ffn_in'], w['w_ffn_out'], w['norm_post_ffn'], l, Bd)
        for g in range(N_GROUPS):
            kv_out[g].append(_kv_rows(a_qkv, Bd, 1, g, 1))
        s_out.append(s_new)
        conv_out.append(jnp.concatenate([conv_state[l][:, 1:], b_qkv[:, None, :]], axis=1))
    return (x2d.reshape(Bd, 1, D_MODEL), jnp.stack(kv_out[0]), jnp.stack(kv_out[1]), jnp.stack(kv_out[2]),
            jnp.stack(s_out), jnp.stack(conv_out))


def kernel(x_prompt, x_sample, cache_kv_w128, cache_kv_w512, cache_kv_w2048, state_delta, state_conv,
           rel_bias, norm_pre_mix, w_in, conv_w, a_log, dt_bias, norm_delta, w_branch_a, w_branch_b,
           w_out, norm_post_mix, norm_pre_ffn, w_ffn_in, w_ffn_out, norm_post_ffn):
    p = dict(w_in=w_in, conv_w=conv_w, a_log=a_log, dt_bias=dt_bias, norm_delta=norm_delta,
             w_branch_a=w_branch_a, w_branch_b=w_branch_b, w_out=w_out, norm_pre_mix=norm_pre_mix,
             norm_post_mix=norm_post_mix, norm_pre_ffn=norm_pre_ffn, w_ffn_in=w_ffn_in,
             w_ffn_out=w_ffn_out, norm_post_ffn=norm_post_ffn)
    w = _prepare(p)
    assert x_sample.shape[1] == 1, "the decode trunk handles one new token per sequence"
    y_p, kv0_p, kv1_p, kv2_p, s_p, conv_p = _trunk_prompt(x_prompt, w, rel_bias)
    y_s, kv0_s, kv1_s, kv2_s, s_s, conv_s = _trunk_sample(
        x_sample, (cache_kv_w128, cache_kv_w512, cache_kv_w2048), state_delta, state_conv, w, rel_bias)
    return (y_p, y_s, kv0_p, kv1_p, kv2_p, s_p, conv_p, kv0_s, kv1_s, kv2_s, s_s, conv_s)
```

```python
import functools

import numpy as np
import jax
import jax.numpy as jnp
from jax import lax
from jax.experimental import pallas as pl
from jax.experimental.pallas import tpu as pltpu

F32 = jnp.float32
BF16 = jnp.bfloat16

D_MODEL = 1024
DEPTH = 4
GROUPS = ((128, 1), (512, 4), (2048, 16))
N_GROUPS = 3
HD_A = 64
H_G = 4
A_OUT = H_G * HD_A
A_GRP = 3 * A_OUT
A_QKV = N_GROUPS * A_GRP
N_KEYS = 129
Q_BLOCK = 128
ATTN_SCALE = HD_A ** -0.5
N_BUCKETS = 32
REL_MAX_DIST = 2048
HB = 8
DK = 128
DV = 128
B_QK = HB * DK
B_V = HB * DV
CONV_W = 4
CONV_DIM = 2 * B_QK + B_V
CHUNK = 64
D_FF = 2816
EPS = 1e-6
NEG = -1e30

COL_B = A_QKV
COL_Z = COL_B + CONV_DIM
COL_BG = COL_Z + B_V
COL_GATE = COL_BG + 2 * HB
N_IN = COL_GATE + 2 * D_MODEL
ZG_DIM = B_V + 2 * D_MODEL
N_MAIN = A_QKV + CONV_DIM + ZG_DIM
PROJ_TN = A_GRP
PROJ_TW = D_MODEL
N_TILES_A = A_QKV // PROJ_TN
N_TILES_B = CONV_DIM // PROJ_TW
N_TILES_ZG = ZG_DIM // PROJ_TW
LANES = 128
FFN_TF = D_FF // 2

VMEM_LIMIT = 52 * 1024 * 1024


VMEM_LIMIT_IN_PROJ = 58 * 1024 * 1024


def _cparams(sem, vmem_limit=VMEM_LIMIT):
    return pltpu.CompilerParams(dimension_semantics=sem, vmem_limit_bytes=vmem_limit)


def _sigmoid(x):
    return 1.0 / (1.0 + jnp.exp(-x))


def _silu(x):
    return x * _sigmoid(x)


def _softplus(x):
    return jnp.maximum(x, 0.0) + jnp.log(1.0 + jnp.exp(-jnp.abs(x)))


def _rms(x, w):
    return x * lax.rsqrt(jnp.mean(x * x, axis=-1, keepdims=True) + EPS) * w


def _dot(a, b):
    return jnp.dot(a.astype(BF16), b.astype(BF16), preferred_element_type=F32)


def _dot_nt(a, b):
    return lax.dot_general(a.astype(BF16), b.astype(BF16), (((1,), (1,)), ((), ())),
                           preferred_element_type=F32)


def _dot_f32(a, b):
    return jnp.dot(a, b, preferred_element_type=F32, precision=lax.Precision.HIGHEST)


def _in_proj_kernel(x_ref, nw_ref, w_ref, wbz_ref, wbg_ref, *refs, prompt):
    if prompt:
        oa_refs, (ob_ref, ozg_ref, obg_ref, kvt_ref, h_ref, slab_ref) = refs[:N_GROUPS], refs[N_GROUPS:]
    else:
        oa_ref, ob_ref, ozg_ref, obg_ref, h_ref = refs
    j = pl.program_id(1)
    tm = x_ref.shape[0]
    n_slabs = PROJ_TN // LANES
    tile = (lambda: w_ref[j]) if len(w_ref.shape) == 3 else (lambda: w_ref[...])
    tile_bz = (lambda: wbz_ref[j - N_TILES_A]) if len(wbz_ref.shape) == 3 else (lambda: wbz_ref[...])

    @pl.when(j == 0)
    def _():
        hb = _rms(x_ref[...], nw_ref[...]).astype(BF16)
        h_ref[...] = hb
        obg_ref[...] = jnp.dot(hb, wbg_ref[...], preferred_element_type=F32)

    def group_tile(g):
        acc = jnp.dot(h_ref[...], tile(), preferred_element_type=F32)
        if not prompt:
            oa_ref[...] = acc
            return
        kvt_ref[...] = acc[:, A_OUT:].T
        dil = GROUPS[g][1]
        qkv = jnp.concatenate([acc[:, :A_OUT] * ATTN_SCALE, acc[:, A_OUT:]], axis=1)
        if dil == 1:
            oa_refs[g][0] = qkv.astype(BF16)
            return
        for c in range(n_slabs):
            slab_ref[c] = qkv[:, c * LANES:(c + 1) * LANES]
        for r in range(dil):
            for c in range(n_slabs):
                rows = slab_ref[c, pl.ds(r, tm // dil, stride=dil), :]
                oa_refs[g][r, :, c * LANES:(c + 1) * LANES] = rows.astype(BF16)

    for g in range(N_TILES_A):
        pl.when(j == g)(functools.partial(group_tile, g))

    @pl.when((j >= N_TILES_A) & (j < N_TILES_A + N_TILES_B))
    def _():
        ob_ref[...] = jnp.dot(h_ref[...], tile_bz(), preferred_element_type=F32)

    @pl.when(j >= N_TILES_A + N_TILES_B)
    def _():
        ozg_ref[...] = jnp.dot(h_ref[...], tile_bz(), preferred_element_type=F32).astype(ozg_ref.dtype)


def _in_proj(x2d, norm_w, w_a, w_bz, w_bg, l, tm, seq_len=None):
    assert A_GRP == PROJ_TN, "one column tile per attention group"
    T = x2d.shape[0]
    n_tiles = N_TILES_A + N_TILES_B + N_TILES_ZG
    last_a, first_zg = N_TILES_A - 1, N_TILES_A + N_TILES_B
    prompt = seq_len is not None
    scratch = [pltpu.VMEM((tm, D_MODEL), BF16)]
    if prompt:
        per_seq = seq_len // tm
        n_seq = T // seq_len
        a_specs = [pl.BlockSpec((None, dil, tm // dil, A_GRP), lambda i, j: (i // per_seq, 0, i % per_seq, 0))
                   for _, dil in GROUPS]
        a_shapes = [jax.ShapeDtypeStruct((n_seq, dil, seq_len // dil, A_GRP), BF16) for _, dil in GROUPS]
        extra_specs = [pl.BlockSpec((None, None, 2 * A_OUT, tm),
                                    lambda i, j: (i // per_seq, jnp.minimum(j, last_a), 0, i % per_seq))]
        extra_shapes = [jax.ShapeDtypeStruct((n_seq, N_GROUPS, 2 * A_OUT, seq_len), F32)]
        scratch.append(pltpu.VMEM((PROJ_TN // LANES, tm, LANES), F32))
        w_specs = [pl.BlockSpec((None,) + w.shape[1:], lambda i, j: (l, 0, 0, 0), pipeline_mode=pl.Buffered(1))
                   for w in (w_a, w_bz)]
        vmem_limit = VMEM_LIMIT_IN_PROJ
    else:
        a_specs = [pl.BlockSpec((tm, PROJ_TN), lambda i, j: (i, jnp.minimum(j, last_a)))]
        a_shapes = [jax.ShapeDtypeStruct((T, A_QKV), F32)]
        extra_specs, extra_shapes = [], []
        w_specs = [
            pl.BlockSpec((None, None, D_MODEL, PROJ_TN), lambda i, j: (l, jnp.minimum(j, last_a), 0, 0)),
            pl.BlockSpec((None, None, D_MODEL, PROJ_TW),
                         lambda i, j: (l, jnp.clip(j - N_TILES_A, 0, N_TILES_B + N_TILES_ZG - 1), 0, 0)),
        ]
        vmem_limit = VMEM_LIMIT
    return pl.pallas_call(
        functools.partial(_in_proj_kernel, prompt=prompt),
        grid=(T // tm, n_tiles),
        in_specs=[
            pl.BlockSpec((tm, D_MODEL), lambda i, j: (i, 0)),
            pl.BlockSpec((None, 1, D_MODEL), lambda i, j: (l, 0, 0)),
        ] + w_specs + [
            pl.BlockSpec((None, D_MODEL, LANES), lambda i, j: (l, 0, 0)),
        ],
        out_specs=a_specs + [
            pl.BlockSpec((None, tm, PROJ_TW), lambda i, j: (jnp.clip(j - N_TILES_A, 0, N_TILES_B - 1), i, 0)),
            pl.BlockSpec((None, tm, PROJ_TW), lambda i, j: (jnp.maximum(j - first_zg, 0), i, 0)),
            pl.BlockSpec((tm, LANES), lambda i, j: (i, 0)),
        ] + extra_specs,
        out_shape=a_shapes + [
            jax.ShapeDtypeStruct((N_TILES_B, T, PROJ_TW), F32),
            jax.ShapeDtypeStruct((N_TILES_ZG, T, PROJ_TW), BF16),
            jax.ShapeDtypeStruct((T, LANES), F32),
        ] + extra_shapes,
        scratch_shapes=scratch,
        compiler_params=_cparams(("parallel", "arbitrary"), vmem_limit),
        name="in_proj",
    )(x2d, norm_w, w_a, w_bz, w_bg)


def _t5_bucket(dist):
    max_exact = N_BUCKETS // 2
    d = np.maximum(dist, 1).astype(np.float32)
    large = max_exact + (np.log(d / max_exact) / np.log(REL_MAX_DIST / max_exact)
                         * (N_BUCKETS - max_exact)).astype(np.int32)
    large = np.minimum(large, N_BUCKETS - 1)
    return np.where(dist < max_exact, dist, large).astype(np.int32)


def _group_bias(rel_bias, g):
    dil = GROUPS[g][1]
    bucket = _t5_bucket(np.arange(N_KEYS) * dil)
    return rel_bias[bucket][:, g * H_G:(g + 1) * H_G].astype(F32)


def _prompt_bias(rel_bias, g):
    period = 3 * Q_BLOCK
    vals = _group_bias(rel_bias, g)
    u = jnp.concatenate([vals[::-1], jnp.full((period - N_KEYS, H_G), NEG, F32)], axis=0).T
    flat = jnp.tile(u, (1, Q_BLOCK))[:, :Q_BLOCK * (period - 1)]
    return flat.reshape(H_G, Q_BLOCK, period - 1)[:, :, :2 * Q_BLOCK]


def _attn_group(g, a_ref, bias_ref, emit):
    dil, lc, _ = a_ref.shape
    nb = lc // Q_BLOCK
    lane_head = lax.broadcasted_iota(jnp.int32, (1, A_OUT), 1) // HD_A
    hms = [lane_head == h for h in range(H_G)]
    blocks = [(r, pb) for r in range(dil) for pb in range(nb)]
    for i in range(0, len(blocks), 2):
        items = blocks[i:i + 2]
        work = [(it, h) for it in items for h in range(H_G)]
        keys = lambda r, pb, part: a_ref[r, pl.ds((pb - 1) * Q_BLOCK, 2 * Q_BLOCK) if pb > 0
                                         else pl.ds(0, Q_BLOCK), part * A_OUT:(part + 1) * A_OUT]
        s = {}
        for (r, pb), h in work:
            q = a_ref[r, pl.ds(pb * Q_BLOCK, Q_BLOCK), 0:A_OUT]
            qh = jnp.where(hms[h], q, jnp.zeros_like(q))
            b = bias_ref[h] if pb > 0 else bias_ref[h, :, Q_BLOCK:2 * Q_BLOCK]
            s[r, pb, h] = _dot_nt(qh, keys(r, pb, 1)) + b
        pn, lse = {}, {}
        for (r, pb), h in work:
            m = jnp.max(s[r, pb, h], axis=-1, keepdims=True)
            p = jnp.exp(s[r, pb, h] - m)
            den = jnp.sum(p, axis=-1, keepdims=True)
            pn[r, pb, h] = (p / den).astype(BF16)
            lse[r, pb, h] = m + jnp.log(den)
        oh = {}
        for (r, pb), h in work:
            oh[r, pb, h] = jnp.dot(pn[r, pb, h], keys(r, pb, 2), preferred_element_type=F32)
        for r, pb in items:
            o_acc, l_acc = oh[r, pb, 0], lse[r, pb, 0]
            for h in range(1, H_G):
                o_acc = jnp.where(hms[h], oh[r, pb, h], o_acc)
                l_acc = jnp.where(hms[h], lse[r, pb, h], l_acc)
            emit(r, pb, o_acc, jnp.broadcast_to(l_acc, (Q_BLOCK, A_OUT)))


def _attn_prompt_kernel(a0_ref, a1_ref, a2_ref, b0_ref, b1_ref, b2_ref, o_ref,
                        od_ref, ld_ref, orun_ref, lrun_ref):
    L = o_ref.shape[0]
    halves = A_OUT // LANES
    for g, (a_ref, bias_ref) in enumerate(((a0_ref, b0_ref), (a1_ref, b1_ref), (a2_ref, b2_ref))):
        dil = a_ref.shape[0]
        lc = L // dil
        if dil == 1:
            assert g == 0

            def emit(r, pb, o, lse):
                for c in range(halves):
                    orun_ref[c, pl.ds(pb * Q_BLOCK, Q_BLOCK), :] = o[:, c * LANES:(c + 1) * LANES]
                    lrun_ref[c, pl.ds(pb * Q_BLOCK, Q_BLOCK), :] = lse[:, c * LANES:(c + 1) * LANES]

            _attn_group(g, a_ref, bias_ref, emit)
            continue

        def emit(r, pb, o, lse, lc=lc):
            od_ref[pl.ds(r * lc + pb * Q_BLOCK, Q_BLOCK), :] = o
            ld_ref[pl.ds(r * lc + pb * Q_BLOCK, Q_BLOCK), :] = lse

        _attn_group(g, a_ref, bias_ref, emit)
        for r in range(dil):
            rows = pl.ds(r, lc, stride=dil)
            src = pl.ds(r * lc, lc)
            for c in range(halves):
                cols = pl.ds(c * LANES, LANES)
                o_new, l_new = od_ref[src, cols], ld_ref[src, cols]
                o_old, l_old = orun_ref[c, rows, :], lrun_ref[c, rows, :]
                m = jnp.maximum(l_old, l_new)
                e_old, e_new = jnp.exp(l_old - m), jnp.exp(l_new - m)
                den = e_old + e_new
                orun_ref[c, rows, :] = (e_old * o_old + e_new * o_new) / den
                if g < N_GROUPS - 1:
                    lrun_ref[c, rows, :] = m + jnp.log(den)
    for c in range(halves):
        o_ref[:, c * LANES:(c + 1) * LANES] = orun_ref[c].astype(o_ref.dtype)


def _attn_prompt(a_groups, biases, B, L):
    halves = A_OUT // LANES
    a_specs = [pl.BlockSpec((None,) + a.shape[1:], lambda b: (b, 0, 0, 0)) for a in a_groups]
    bias_spec = pl.BlockSpec((H_G, Q_BLOCK, 2 * Q_BLOCK), lambda b: (0, 0, 0))
    return pl.pallas_call(
        _attn_prompt_kernel,
        grid=(B,),
        in_specs=a_specs + [bias_spec] * N_GROUPS,
        out_specs=pl.BlockSpec((L, A_OUT), lambda b: (b, 0)),
        out_shape=jax.ShapeDtypeStruct((B * L, A_OUT), BF16),
        scratch_shapes=[pltpu.VMEM((L, A_OUT), F32), pltpu.VMEM((L, A_OUT), F32),
                        pltpu.VMEM((halves, L, LANES), F32), pltpu.VMEM((halves, L, LANES), F32)],
        compiler_params=_cparams(("parallel",)),
        name="attn_prompt",
    )(*a_groups, *biases)


def _delta_prompt_kernel(bq_ref, bk_ref, bv_ref, z_ref, bg_ref, cw_ref, hp_ref, nd_ref,
                         ob_ref, sout_ref, s_ref, xc_ref):
    c = pl.program_id(1)
    C = CHUNK
    n_seq = bq_ref.shape[0]

    @pl.when(c == 0)
    def _():
        s_ref[...] = jnp.zeros_like(s_ref)
        xc_ref[:, 0:8, :] = jnp.zeros((n_seq, 8, CONV_DIM), F32)

    xc_ref[:, 8:8 + C, 0:B_QK] = bq_ref[...]
    xc_ref[:, 8:8 + C, B_QK:2 * B_QK] = bk_ref[...]
    xc_ref[:, 8:8 + C, 2 * B_QK:CONV_DIM] = bv_ref[...]

    def conv(i, lo):
        y = cw_ref[0:1, lo:lo + DK] * xc_ref[i, 5:5 + C, lo:lo + DK]
        for j in range(1, CONV_W):
            y = y + cw_ref[j:j + 1, lo:lo + DK] * xc_ref[i, 5 + j:5 + j + C, lo:lo + DK]
        return _silu(y)

    hp = hp_ref[...]
    row = lax.broadcasted_iota(jnp.int32, (C, C), 0)
    colc = lax.broadcasted_iota(jnp.int32, (C, C), 1)
    tril = row >= colc
    tril_f = tril.astype(F32)
    eye = (row == colc).astype(F32)
    r2 = lax.broadcasted_iota(jnp.int32, (LANES, LANES), 0)
    c2 = lax.broadcasted_iota(jnp.int32, (LANES, LANES), 1)
    eye_l = (r2 == c2).astype(F32)
    nd = nd_ref[...]

    units = [(i, h) for i in range(n_seq) for h in range(HB)]
    idx = range(len(units))
    ks, kqs, vbs, kbgs, qgs, kdts, decays, elasts = [], [], [], [], [], [], [], []
    for i in range(n_seq):
        bg = bg_ref[i]
        beta_all = _sigmoid(bg)
        g_all = -jnp.exp(hp[0:1, :]) * _softplus(bg + hp[1:2, :])
        gc = _dot_f32(tril_f, g_all)
        gct = lax.dot_general(eye_l, gc, (((1,), (1,)), ((), ())), preferred_element_type=F32,
                              precision=lax.Precision.HIGHEST)
        for h in range(HB):
            lo = h * DK
            q = conv(i, lo)
            k = conv(i, B_QK + lo)
            v = conv(i, 2 * B_QK + lo)
            q = q * (lax.rsqrt(jnp.sum(q * q, axis=-1, keepdims=True) + EPS) * (DK ** -0.5))
            k = k * lax.rsqrt(jnp.sum(k * k, axis=-1, keepdims=True) + EPS)
            beta = beta_all[:, h:h + 1]
            gcol = gc[:, HB + h:HB + h + 1]
            grow = gct[HB + h:HB + h + 1, :]
            glast = gc[C - 1:C, HB + h:HB + h + 1]
            egc = jnp.exp(gcol)
            kb = k * beta
            ks.append(k.astype(BF16))
            kqs.append(jnp.concatenate([kb, q], axis=0).astype(BF16))
            vbs.append(v * beta)
            kbgs.append(kb * egc)
            qgs.append(q * egc)
            kdts.append((k * jnp.exp(glast - gcol)).T)
            decays.append(jnp.exp(jnp.where(tril, gcol - grow, NEG)))
            elasts.append(jnp.exp(glast))
    kq = [_dot_nt(kqs[u], ks[u]) for u in idx]
    pw = [jnp.where(row > colc, kq[u][0:C] * decays[u], 0.0) for u in idx]
    attn = [kq[u][C:2 * C] * decays[u] for u in idx]
    p = [eye - pw[u] for u in idx]
    pw = [_dot(pw[u], pw[u]) for u in idx]
    n = 4
    while n < C:
        nxt = [_dot(pw[u], pw[u]) for u in idx]
        p = [p[u] + _dot(p[u], pw[u]) for u in idx]
        pw = nxt
        n *= 2
    p = [p[u] + _dot(p[u], pw[u]) for u in idx]
    uw = [_dot(p[u], jnp.concatenate([vbs[u], kbgs[u]], axis=1)) for u in idx]
    s_old = [s_ref[i, h] for i, h in units]
    ws_qs = [_dot(jnp.concatenate([uw[u][:, DV:DV + DK], qgs[u]], axis=0), s_old[u]) for u in idx]
    v_new = [uw[u][:, 0:DV] - ws_qs[u][0:C] for u in idx]
    av = [_dot(jnp.concatenate([attn[u], kdts[u]], axis=0), v_new[u]) for u in idx]
    for u, (i, h) in enumerate(units):
        lo = h * DK
        s_ref[i, h] = s_old[u] * elasts[u] + av[u][C:C + DK]
        o = ws_qs[u][C:2 * C] + av[u][0:C]
        o = o * lax.rsqrt(jnp.mean(o * o, axis=-1, keepdims=True) + EPS) * nd
        ob_ref[i, :, lo:lo + DV] = (o * _silu(z_ref[i, :, lo:lo + DV].astype(F32))).astype(ob_ref.dtype)

    xc_ref[:, 0:8, :] = xc_ref[:, C:C + 8, :]

    @pl.when(c == pl.num_programs(1) - 1)
    def _():
        sout_ref[...] = s_ref[...]


DELTA_SEQS = 4


def _delta_prompt(b_qkv, zg, bg, conv_w, head_params, norm_delta, l, B, L):
    n = L // CHUNK
    ns = DELTA_SEQS if B % DELTA_SEQS == 0 else 1
    seq = lambda t: t.reshape(t.shape[:-2] + (B, L, t.shape[-1]))
    rows = lambda w, part: pl.BlockSpec((ns, CHUNK, w), lambda b, c: (b, c, part))
    slab = lambda part: pl.BlockSpec((None, ns, CHUNK, PROJ_TW), lambda b, c: (part, b, c, 0))
    o_b, s_fin = pl.pallas_call(
        _delta_prompt_kernel,
        grid=(B // ns, n),
        in_specs=[
            slab(0), slab(1), slab(2),
            slab(0),
            rows(LANES, 0),
            pl.BlockSpec((None, CONV_W, CONV_DIM), lambda b, c: (l, 0, 0)),
            pl.BlockSpec((None, 2, LANES), lambda b, c: (l, 0, 0)),
            pl.BlockSpec((None, 1, DV), lambda b, c: (l, 0, 0)),
        ],
        out_specs=[
            rows(B_V, 0),
            pl.BlockSpec((ns, HB, DK, DV), lambda b, c: (b, 0, 0, 0)),
        ],
        out_shape=[
            jax.ShapeDtypeStruct((B, L, B_V), BF16),
            jax.ShapeDtypeStruct((B, HB, DK, DV), F32),
        ],
        scratch_shapes=[pltpu.VMEM((ns, HB, DK, DV), F32), pltpu.VMEM((ns, 8 + CHUNK, CONV_DIM), F32)],
        compiler_params=_cparams(("parallel", "arbitrary")),
        name="delta_prompt",
    )(seq(b_qkv), seq(b_qkv), seq(b_qkv), seq(zg), seq(bg), conv_w, head_params, norm_delta)
    return o_b.reshape(B * L, B_V), s_fin


def _merge_groups(os_, lses):
    m = jnp.maximum(jnp.maximum(lses[0], lses[1]), lses[2])
    es = [jnp.exp(t - m) for t in lses]
    den = es[0] + es[1] + es[2]
    return (es[0] * os_[0] + es[1] * os_[1] + es[2] * os_[2]) / den


def _mix_kernel(oa_ref, ob_ref, ga_ref, gb_ref, x_ref, wa_ref, wb_ref, wo_ref, nw_ref, out_ref):
    ya = _dot(oa_ref[...], wa_ref[...])
    yb = jnp.dot(ob_ref[...], wb_ref[...], preferred_element_type=F32)
    mix = _sigmoid(ga_ref[...].astype(F32)) * ya + _sigmoid(gb_ref[...].astype(F32)) * yb
    y = _dot(mix, wo_ref[...])
    out_ref[...] = x_ref[...] + _rms(y, nw_ref[...])


def _mix(o_a, o_b, zg, x2d, wa, wb, wo, norm_w, l, tm):
    T = x2d.shape[0]
    rows = lambda w: pl.BlockSpec((tm, w), lambda i: (i, 0))
    return pl.pallas_call(
        _mix_kernel,
        grid=(T // tm,),
        in_specs=[
            rows(A_OUT),
            rows(B_V),
            pl.BlockSpec((None, tm, D_MODEL), lambda i: (1, i, 0)),
            pl.BlockSpec((None, tm, D_MODEL), lambda i: (2, i, 0)),
            rows(D_MODEL),
            pl.BlockSpec((None, A_OUT, D_MODEL), lambda i: (l, 0, 0)),
            pl.BlockSpec((None, B_V, D_MODEL), lambda i: (l, 0, 0)),
            pl.BlockSpec((None, D_MODEL, D_MODEL), lambda i: (l, 0, 0)),
            pl.BlockSpec((None, 1, D_MODEL), lambda i: (l, 0, 0)),
        ],
        out_specs=rows(D_MODEL),
        out_shape=jax.ShapeDtypeStruct((T, D_MODEL), F32),
        compiler_params=_cparams(("parallel",)),
        name="mix",
    )(o_a, o_b, zg, zg, x2d, wa, wb, wo, norm_w)


def _ffn_kernel(x_ref, nw1_ref, wg_ref, wu_ref, wd_ref, nw2_ref, out_ref, h_ref, acc_ref):
    j = pl.program_id(1)

    @pl.when(j == 0)
    def _():
        h_ref[...] = _rms(x_ref[...], nw1_ref[...]).astype(BF16)
        acc_ref[...] = jnp.zeros_like(acc_ref)

    h = h_ref[...]
    gt = jnp.dot(h, wg_ref[...], preferred_element_type=F32)
    up = jnp.dot(h, wu_ref[...], preferred_element_type=F32)
    acc_ref[...] += _dot(_silu(gt) * up, wd_ref[...])

    @pl.when(j == pl.num_programs(1) - 1)
    def _():
        out_ref[...] = x_ref[...] + _rms(acc_ref[...], nw2_ref[...])


def _ffn(x2d, nw1, w_in, w_out, nw2, l, tm):
    T = x2d.shape[0]
    nf = D_FF // FFN_TF
    return pl.pallas_call(
        _ffn_kernel,
        grid=(T // tm, nf),
        in_specs=[
            pl.BlockSpec((tm, D_MODEL), lambda i, j: (i, 0)),
            pl.BlockSpec((None, 1, D_MODEL), lambda i, j: (l, 0, 0)),
            pl.BlockSpec((None, D_MODEL, FFN_TF), lambda i, j: (l, 0, j)),
            pl.BlockSpec((None, D_MODEL, FFN_TF), lambda i, j: (l, 0, nf + j)),
            pl.BlockSpec((None, FFN_TF, D_MODEL), lambda i, j: (l, j, 0)),
            pl.BlockSpec((None, 1, D_MODEL), lambda i, j: (l, 0, 0)),
        ],
        out_specs=pl.BlockSpec((tm, D_MODEL), lambda i, j: (i, 0)),
        out_shape=jax.ShapeDtypeStruct((T, D_MODEL), F32),
        scratch_shapes=[pltpu.VMEM((tm, D_MODEL), BF16), pltpu.VMEM((tm, D_MODEL), F32)],
        compiler_params=_cparams(("parallel", "arbitrary")),
        name="ffn",
    )(x2d, nw1, w_in, w_in, w_out, nw2)


N_CACHED = N_KEYS - 1


def _attn_sample_kernel(a_ref, c0_ref, c1_ref, c2_ref, b0_ref, b1_ref, b2_ref, bnew_ref, seg_ref, o_ref):
    caches = (c0_ref, c1_ref, c2_ref)
    biases = (b0_ref, b1_ref, b2_ref)
    seg = seg_ref[...]
    a = a_ref[0]
    gh = [(g, h) for g in range(N_GROUPS) for h in range(H_G)]
    rows8 = lambda t: jnp.broadcast_to(t, (8, t.shape[-1]))
    q, vn, s_new = [], [], []
    for g in range(N_GROUPS):
        lo = g * A_GRP
        q.append(a[:, lo:lo + A_OUT] * ATTN_SCALE)
        vn.append(a[:, lo + 2 * A_OUT:lo + 3 * A_OUT])
        s_new.append(_dot(rows8(q[g] * a[:, lo + A_OUT:lo + 2 * A_OUT]), seg) + bnew_ref[g])
    head = lambda t, h: t[:, h * HD_A:(h + 1) * HD_A]
    s = {(g, h): _dot(rows8(head(q[g], h)), caches[g][0, h]) + biases[g][h:h + 1, :] for g, h in gh}
    pn, p_new, lse = {}, {}, {}
    for g, h in gh:
        sn = s_new[g][:, h * HD_A:h * HD_A + 1]
        m = jnp.maximum(jnp.max(s[g, h], axis=-1, keepdims=True), sn)
        p = jnp.exp(s[g, h] - m)
        e_new = jnp.exp(sn - m)
        den = jnp.sum(p, axis=-1, keepdims=True) + e_new
        pn[g, h] = p / den
        p_new[g, h] = e_new / den
        lse[g, h] = m + jnp.log(den)
    o = {(g, h): _dot_nt(pn[g, h], caches[g][1, h]) + p_new[g, h] * head(vn[g], h) for g, h in gh}
    merged = [_merge_groups([o[g, h] for g in range(N_GROUPS)], [lse[g, h] for g in range(N_GROUPS)])
              for h in range(H_G)]
    o_ref[0] = jnp.concatenate(merged, axis=1)[0:1, :]


def _attn_sample(a_qkv, caches_t, biases, bias_new, seg, l):
    Bd = a_qkv.shape[0]
    cache_specs = [pl.BlockSpec((None, None) + c.shape[2:], lambda i: (l, i, 0, 0, 0, 0)) for c in caches_t]
    bias_specs = [pl.BlockSpec(b.shape, lambda i: (0, 0)) for b in biases]
    o = pl.pallas_call(
        _attn_sample_kernel,
        grid=(Bd,),
        in_specs=[pl.BlockSpec((1, 1, A_QKV), lambda i: (i, 0, 0))] + cache_specs + bias_specs + [
            pl.BlockSpec((N_GROUPS, 1, A_OUT), lambda i: (0, 0, 0)),
            pl.BlockSpec((A_OUT, A_OUT), lambda i: (0, 0)),
        ],
        out_specs=pl.BlockSpec((1, 1, A_OUT), lambda i: (i, 0, 0)),
        out_shape=jax.ShapeDtypeStruct((Bd, 1, A_OUT), F32),
        compiler_params=_cparams(("parallel",)),
        name="attn_sample",
    )(a_qkv.reshape(Bd, 1, A_QKV), *caches_t, *biases, bias_new, seg)
    return o.reshape(Bd, A_OUT)


def _sample_bias(rel_bias):
    rows, new = [], []
    for g, (win, dil) in enumerate(GROUPS):
        vals = _group_bias(rel_bias, g)
        sel = vals[1:][::-1][:, None, :]
        skip = jnp.full((N_CACHED, dil - 1, H_G), NEG, F32)
        rows.append(jnp.concatenate([sel, skip], axis=1).reshape(N_CACHED * dil, H_G).T)
        new.append(jnp.repeat(vals[0:1], HD_A, axis=1))
    return rows, jnp.stack(new)


def _delta_sample_kernel(bn_ref, cb_ref, z_ref, bg_ref, s_ref, cw_ref, hp_ref, nd_ref, ob_ref, sout_ref):
    bb = bn_ref.shape[0]
    bg = bg_ref[...]
    hp = hp_ref[...]
    beta_all = _sigmoid(bg)
    g_all = -jnp.exp(hp[0:1, :]) * _softplus(bg + hp[1:2, :])
    eg_all = jnp.exp(g_all)
    nd = nd_ref[...]
    r8 = lax.broadcasted_iota(jnp.int32, (1, 8, 1), 1)
    r2 = lax.broadcasted_iota(jnp.int32, (DK, DK), 0)
    c2 = lax.broadcasted_iota(jnp.int32, (DK, DK), 1)
    eye = jnp.broadcast_to((r2 == c2).astype(BF16)[None], (bb, DK, DK))

    def conv(lo):
        y = cw_ref[CONV_W - 1:CONV_W, lo:lo + DK] * bn_ref[:, :, lo:lo + DK]
        for j in range(CONV_W - 1):
            y = y + cw_ref[j:j + 1, lo:lo + DK] * cb_ref[:, j:j + 1, lo:lo + DK]
        return _silu(y)

    for h in range(HB):
        lo = h * DK
        q = conv(lo)
        k = conv(B_QK + lo)
        v = conv(2 * B_QK + lo)
        q = q * lax.rsqrt(jnp.sum(q * q, axis=-1, keepdims=True) + EPS) * (DK ** -0.5)
        k = k * lax.rsqrt(jnp.sum(k * k, axis=-1, keepdims=True) + EPS)
        beta = beta_all[:, :, h:h + 1]
        eg = eg_all[:, :, HB + h:HB + h + 1]
        s = s_ref[:, h]
        w = k * (beta * eg)
        qg = q * eg
        lhs = jnp.where(r8 == 0, w, jnp.where(r8 == 1, qg, 0.0))
        r = jnp.einsum('bmk,bkv->bmv', lhs.astype(BF16), s.astype(BF16), preferred_element_type=F32)
        v_new = v * beta - r[:, 0:1, :]
        attn = jnp.sum(q * k, axis=-1, keepdims=True)
        o = r[:, 1:2, :] + attn * v_new
        kt = jnp.einsum('bij,bmj->bim', eye, jnp.broadcast_to(k, (bb, 8, DK)).astype(BF16),
                        preferred_element_type=F32)
        sout_ref[:, h] = s * eg + kt[:, :, 0:1] * v_new
        o = o * lax.rsqrt(jnp.mean(o * o, axis=-1, keepdims=True) + EPS) * nd
        ob_ref[:, :, lo:lo + DV] = (o * _silu(z_ref[:, :, lo:lo + DV].astype(F32))).astype(ob_ref.dtype)


def _delta_sample(b_qkv, conv_state, zg, bg, state, conv_w, head_params, norm_delta, l, bb):
    Bd = b_qkv.shape[0]
    row = lambda w: pl.BlockSpec((bb, 1, w), lambda i: (i, 0, 0))
    o_b, s_new = pl.pallas_call(
        _delta_sample_kernel,
        grid=(Bd // bb,),
        in_specs=[
            row(CONV_DIM),
            pl.BlockSpec((None, bb, CONV_W - 1, CONV_DIM), lambda i: (l, i, 0, 0)),
            row(B_V),
            row(LANES),
            pl.BlockSpec((None, bb, HB, DK, DV), lambda i: (l, i, 0, 0, 0)),
            pl.BlockSpec((None, CONV_W, CONV_DIM), lambda i: (l, 0, 0)),
            pl.BlockSpec((None, 2, LANES), lambda i: (l, 0, 0)),
            pl.BlockSpec((None, 1, DV), lambda i: (l, 0, 0)),
        ],
        out_specs=[
            row(B_V),
            pl.BlockSpec((bb, HB, DK, DV), lambda i: (i, 0, 0, 0)),
        ],
        out_shape=[
            jax.ShapeDtypeStruct((Bd, 1, B_V), BF16),
            jax.ShapeDtypeStruct((Bd, HB, DK, DV), F32),
        ],
        compiler_params=_cparams(("parallel",)),
        name="delta_sample",
    )(b_qkv.reshape(Bd, 1, CONV_DIM), conv_state, zg[0].reshape(Bd, 1, B_V), bg.reshape(Bd, 1, LANES),
      state, conv_w, head_params, norm_delta)
    return o_b.reshape(Bd, B_V), s_new


def _prepare(p):
    w_in = p['w_in']
    tiles = lambda t, width: jnp.transpose(
        t.astype(BF16).reshape(DEPTH, D_MODEL, t.shape[-1] // width, width), (0, 2, 1, 3))
    w_a = tiles(w_in[:, :, :COL_B], PROJ_TN)
    w_bz = tiles(jnp.concatenate([w_in[:, :, COL_B:COL_BG], w_in[:, :, COL_GATE:]], axis=-1), PROJ_TW)
    w_bg = jnp.pad(w_in[:, :, COL_BG:COL_GATE], ((0, 0), (0, 0), (0, LANES - 2 * HB))).astype(BF16)
    pad = lambda t: jnp.pad(t.astype(F32), ((0, 0), (HB, LANES - 2 * HB)))
    head_params = jnp.stack([pad(p['a_log']), pad(p['dt_bias'])], axis=1)
    vec = lambda t: t.astype(F32)[:, None, :]
    return dict(
        w_a=w_a, w_bz=w_bz, w_bg=w_bg, head_params=head_params,
        conv_w=p['conv_w'].astype(F32),
        wa=p['w_branch_a'].astype(BF16), wb=p['w_branch_b'].astype(BF16), wo=p['w_out'].astype(BF16),
        w_ffn_in=p['w_ffn_in'].astype(BF16), w_ffn_out=p['w_ffn_out'].astype(BF16),
        norm_pre_mix=vec(p['norm_pre_mix']), norm_post_mix=vec(p['norm_post_mix']),
        norm_pre_ffn=vec(p['norm_pre_ffn']), norm_post_ffn=vec(p['norm_post_ffn']),
        norm_delta=vec(p['norm_delta']),
    )


def _kv_rows(a_qkv, B, L, g, rows):
    t = a_qkv.reshape(B, L, N_GROUPS, 3, H_G, HD_A)
    return t[:, L - rows:, g, 1:]


def _trunk_prompt(x, w, rel_bias):
    B, L, _ = x.shape
    T = B * L
    tm = min(1024, T)
    x2d = x.reshape(T, D_MODEL)
    biases = [_prompt_bias(rel_bias, g) for g in range(N_GROUPS)]
    kv_out = [[] for _ in range(N_GROUPS)]
    s_out, conv_out = [], []
    for l in range(DEPTH):
        a0, a1, a2, b_qkv, zg, bg, kvt = _in_proj(x2d, w['norm_pre_mix'], w['w_a'], w['w_bz'], w['w_bg'], l, tm,
                                                  seq_len=L)
        o_a = _attn_prompt([a0, a1, a2], biases, B, L)
        o_b, s_fin = _delta_prompt(b_qkv, zg, bg, w['conv_w'], w['head_params'], w['norm_delta'], l, B, L)
        x2d = _mix(o_a, o_b, zg, x2d, w['wa'], w['wb'], w['wo'], w['norm_post_mix'], l, min(256, T))
        x2d = _ffn(x2d, w['norm_pre_ffn'], w['w_ffn_in'], w['w_ffn_out'], w['norm_post_ffn'], l, min(512, T))
        for g, (win, _) in enumerate(GROUPS):
            rows = min(win, L)
            kv_out[g].append(kvt[:, g, :, L - rows:].reshape(B, 2, H_G, HD_A, rows))
        s_out.append(s_fin)
        tail = b_qkv.reshape(N_TILES_B, B, L, PROJ_TW)[:, :, L - (CONV_W - 1):]
        conv_out.append(jnp.transpose(tail, (1, 2, 0, 3)).reshape(B, CONV_W - 1, CONV_DIM))
    kv = [jnp.transpose(jnp.stack(t), (0, 1, 5, 2, 3, 4)) for t in kv_out]
    return (x2d.reshape(B, L, D_MODEL), kv[0], kv[1], kv[2], jnp.stack(s_out), jnp.stack(conv_out))


def _trunk_sample(x, caches, state, conv_state, w, rel_bias):
    Bd = x.shape[0]
    x2d = x.reshape(Bd, D_MODEL)
    biases, bias_new = _sample_bias(rel_bias)
    lane_head = np.arange(A_OUT) // HD_A
    seg = jnp.asarray(lane_head[:, None] == lane_head[None, :], BF16)
    bb = min(8, Bd)
    for c, (win, dil) in zip(caches, GROUPS):
        assert c.shape[2] == N_CACHED * dil, "cache length must equal the group's window"
    caches_t = [jnp.transpose(c, (0, 1, 3, 4, 5, 2)) for c in caches]
    kv_out = [[] for _ in range(N_GROUPS)]
    s_out, conv_out = [], []
    for l in range(DEPTH):
        a_qkv, b_slabs, zg, bg = _in_proj(x2d, w['norm_pre_mix'], w['w_a'], w['w_bz'], w['w_bg'], l, Bd)
        b_qkv = jnp.transpose(b_slabs, (1, 0, 2)).reshape(Bd, CONV_DIM)
        o_a = _attn_sample(a_qkv, caches_t, biases, bias_new, seg, l)
        o_b, s_new = _delta_sample(b_qkv, conv_state, zg, bg, state, w['conv_w'], w['head_params'],
                                   w['norm_delta'], l, bb)
        x2d = _mix(o_a, o_b, zg, x2d, w['wa'], w['wb'], w['wo'], w['norm_post_mix'], l, Bd)
        x2d = _ffn(x2d, w['norm_pre_ffn'], w['w_ffn_in'], w['w_ffn_out'], w['norm_post_ffn'], l, Bd)
        for g in range(N_GROUPS):
            kv_out[g].append(_kv_rows(a_qkv, Bd, 1, g, 1))
        s_out.append(s_new)
        conv_out.append(jnp.concatenate([conv_state[l][:, 1:], b_qkv[:, None, :]], axis=1))
    return (x2d.reshape(Bd, 1, D_MODEL), jnp.stack(kv_out[0]), jnp.stack(kv_out[1]), jnp.stack(kv_out[2]),
            jnp.stack(s_out), jnp.stack(conv_out))


def kernel(x_prompt, x_sample, cache_kv_w128, cache_kv_w512, cache_kv_w2048, state_delta, state_conv,
           rel_bias, norm_pre_mix, w_in, conv_w, a_log, dt_bias, norm_delta, w_branch_a, w_branch_b,
           w_out, norm_post_mix, norm_pre_ffn, w_ffn_in, w_ffn_out, norm_post_ffn):
    p = dict(w_in=w_in, conv_w=conv_w, a_log=a_log, dt_bias=dt_bias, norm_delta=norm_delta,
             w_branch_a=w_branch_a, w_branch_b=w_branch_b, w_out=w_out, norm_pre_mix=norm_pre_mix,
             norm_post_mix=norm_post_mix, norm_pre_ffn=norm_pre_ffn, w_ffn_in=w_ffn_in,
             w_ffn_out=w_ffn_out, norm_post_ffn=norm_post_ffn)
    w = _prepare(p)
    assert x_sample.shape[1] == 1, "the decode trunk handles one new token per sequence"
    y_p, kv0_p, kv1_p, kv2_p, s_p, conv_p = _trunk_prompt(x_prompt, w, rel_bias)
    y_s, kv0_s, kv1_s, kv2_s, s_s, conv_s = _trunk_sample(
        x_sample, (cache_kv_w128, cache_kv_w512, cache_kv_w2048), state_delta, state_conv, w, rel_bias)
    return (y_p, y_s, kv0_p, kv1_p, kv2_p, s_p, conv_p, kv0_s, kv1_s, kv2_s, s_s, conv_s)
```

```python
import functools

import numpy as np
import jax
import jax.numpy as jnp
from jax import lax
from jax.experimental import pallas as pl
from jax.experimental.pallas import tpu as pltpu

F32 = jnp.float32
BF16 = jnp.bfloat16

D_MODEL = 1024
DEPTH = 4
GROUPS = ((128, 1), (512, 4), (2048, 16))
N_GROUPS = 3
HD_A = 64
H_G = 4
A_OUT = H_G * HD_A
A_GRP = 3 * A_OUT
A_QKV = N_GROUPS * A_GRP
N_KEYS = 129
Q_BLOCK = 128
ATTN_SCALE = HD_A ** -0.5
N_BUCKETS = 32
REL_MAX_DIST = 2048
HB = 8
DK = 128
DV = 128
B_QK = HB * DK
B_V = HB * DV
CONV_W = 4
CONV_DIM = 2 * B_QK + B_V
CHUNK = 64
D_FF = 2816
EPS = 1e-6
NEG = -1e30

COL_B = A_QKV
COL_Z = COL_B + CONV_DIM
COL_BG = COL_Z + B_V
COL_GATE = COL_BG + 2 * HB
N_IN = COL_GATE + 2 * D_MODEL
ZG_DIM = B_V + 2 * D_MODEL
N_MAIN = A_QKV + CONV_DIM + ZG_DIM
PROJ_TN = A_GRP
PROJ_TW = D_MODEL
N_TILES_A = A_QKV // PROJ_TN
N_TILES_B = CONV_DIM // PROJ_TW
N_TILES_ZG = ZG_DIM // PROJ_TW
LANES = 128
FFN_TF = D_FF // 2

VMEM_LIMIT = 52 * 1024 * 1024


VMEM_LIMIT_IN_PROJ = 58 * 1024 * 1024


def _cparams(sem, vmem_limit=VMEM_LIMIT):
    return pltpu.CompilerParams(dimension_semantics=sem, vmem_limit_bytes=vmem_limit)


def _sigmoid(x):
    return 1.0 / (1.0 + jnp.exp(-x))


def _silu(x):
    return x * _sigmoid(x)


def _softplus(x):
    return jnp.maximum(x, 0.0) + jnp.log(1.0 + jnp.exp(-jnp.abs(x)))


def _rms(x, w):
    return x * lax.rsqrt(jnp.mean(x * x, axis=-1, keepdims=True) + EPS) * w


def _dot(a, b):
    return jnp.dot(a.astype(BF16), b.astype(BF16), preferred_element_type=F32)


def _dot_nt(a, b):
    return lax.dot_general(a.astype(BF16), b.astype(BF16), (((1,), (1,)), ((), ())),
                           preferred_element_type=F32)


def _dot_f32(a, b):
    return jnp.dot(a, b, preferred_element_type=F32, precision=lax.Precision.HIGHEST)


def _in_proj_kernel(x_ref, nw_ref, w_ref, wbz_ref, wbg_ref, *refs, prompt):
    if prompt:
        oa_refs, (ob_ref, ozg_ref, obg_ref, kvt_ref, h_ref, slab_ref) = refs[:N_GROUPS], refs[N_GROUPS:]
    else:
        oa_ref, ob_ref, ozg_ref, obg_ref, h_ref = refs
    j = pl.program_id(1)
    tm = x_ref.shape[0]
    n_slabs = PROJ_TN // LANES
    tile = (lambda: w_ref[j]) if len(w_ref.shape) == 3 else (lambda: w_ref[...])
    tile_bz = (lambda: wbz_ref[j - N_TILES_A]) if len(wbz_ref.shape) == 3 else (lambda: wbz_ref[...])

    @pl.when(j == 0)
    def _():
        hb = _rms(x_ref[...], nw_ref[...]).astype(BF16)
        h_ref[...] = hb
        obg_ref[...] = jnp.dot(hb, wbg_ref[...], preferred_element_type=F32)

    def group_tile(g):
        acc = jnp.dot(h_ref[...], tile(), preferred_element_type=F32)
        if not prompt:
            oa_ref[...] = acc
            return
        kvt_ref[...] = acc[:, A_OUT:].T
        dil = GROUPS[g][1]
        qkv = jnp.concatenate([acc[:, :A_OUT] * ATTN_SCALE, acc[:, A_OUT:]], axis=1)
        if dil == 1:
            oa_refs[g][0] = qkv.astype(BF16)
            return
        for c in range(n_slabs):
            slab_ref[c] = qkv[:, c * LANES:(c + 1) * LANES]
        for r in range(dil):
            for c in range(n_slabs):
                rows = slab_ref[c, pl.ds(r, tm // dil, stride=dil), :]
                oa_refs[g][r, :, c * LANES:(c + 1) * LANES] = rows.astype(BF16)

    for g in range(N_TILES_A):
        pl.when(j == g)(functools.partial(group_tile, g))

    @pl.when((j >= N_TILES_A) & (j < N_TILES_A + N_TILES_B))
    def _():
        ob_ref[...] = jnp.dot(h_ref[...], tile_bz(), preferred_element_type=F32)

    @pl.when(j >= N_TILES_A + N_TILES_B)
    def _():
        ozg_ref[...] = jnp.dot(h_ref[...], tile_bz(), preferred_element_type=F32).astype(ozg_ref.dtype)


def _in_proj(x2d, norm_w, w_a, w_bz, w_bg, l, tm, seq_len=None):
    assert A_GRP == PROJ_TN, "one column tile per attention group"
    T = x2d.shape[0]
    n_tiles = N_TILES_A + N_TILES_B + N_TILES_ZG
    last_a, first_zg = N_TILES_A - 1, N_TILES_A + N_TILES_B
    prompt = seq_len is not None
    scratch = [pltpu.VMEM((tm, D_MODEL), BF16)]
    if prompt:
        per_seq = seq_len // tm
        n_seq = T // seq_len
        a_specs = [pl.BlockSpec((None, dil, tm // dil, A_GRP), lambda i, j: (i // per_seq, 0, i % per_seq, 0))
                   for _, dil in GROUPS]
        a_shapes = [jax.ShapeDtypeStruct((n_seq, dil, seq_len // dil, A_GRP), BF16) for _, dil in GROUPS]
        extra_specs = [pl.BlockSpec((None, None, 2 * A_OUT, tm),
                                    lambda i, j: (i // per_seq, jnp.minimum(j, last_a), 0, i % per_seq))]
        extra_shapes = [jax.ShapeDtypeStruct((n_seq, N_GROUPS, 2 * A_OUT, seq_len), F32)]
        scratch.append(pltpu.VMEM((PROJ_TN // LANES, tm, LANES), F32))
        w_specs = [pl.BlockSpec((None,) + w.shape[1:], lambda i, j: (l, 0, 0, 0), pipeline_mode=pl.Buffered(1))
                   for w in (w_a, w_bz)]
        vmem_limit = VMEM_LIMIT_IN_PROJ
    else:
        a_specs = [pl.BlockSpec((tm, PROJ_TN), lambda i, j: (i, jnp.minimum(j, last_a)))]
        a_shapes = [jax.ShapeDtypeStruct((T, A_QKV), F32)]
        extra_specs, extra_shapes = [], []
        w_specs = [
            pl.BlockSpec((None, None, D_MODEL, PROJ_TN), lambda i, j: (l, jnp.minimum(j, last_a), 0, 0)),
            pl.BlockSpec((None, None, D_MODEL, PROJ_TW),
                         lambda i, j: (l, jnp.clip(j - N_TILES_A, 0, N_TILES_B + N_TILES_ZG - 1), 0, 0)),
        ]
        vmem_limit = VMEM_LIMIT
    return pl.pallas_call(
        functools.partial(_in_proj_kernel, prompt=prompt),
        grid=(T // tm, n_tiles),
        in_specs=[
            pl.BlockSpec((tm, D_MODEL), lambda i, j: (i, 0)),
            pl.BlockSpec((None, 1, D_MODEL), lambda i, j: (l, 0, 0)),
        ] + w_specs + [
            pl.BlockSpec((None, D_MODEL, LANES), lambda i, j: (l, 0, 0)),
        ],
        out_specs=a_specs + [
            pl.BlockSpec((None, tm, PROJ_TW), lambda i, j: (jnp.clip(j - N_TILES_A, 0, N_TILES_B - 1), i, 0)),
            pl.BlockSpec((None, tm, PROJ_TW), lambda i, j: (jnp.maximum(j - first_zg, 0), i, 0)),
            pl.BlockSpec((tm, LANES), lambda i, j: (i, 0)),
        ] + extra_specs,
        out_shape=a_shapes + [
            jax.ShapeDtypeStruct((N_TILES_B, T, PROJ_TW), F32),
            jax.ShapeDtypeStruct((N_TILES_ZG, T, PROJ_TW), BF16),
            jax.ShapeDtypeStruct((T, LANES), F32),
        ] + extra_shapes,
        scratch_shapes=scratch,
        compiler_params=_cparams(("parallel", "arbitrary"), vmem_limit),
        name="in_proj",
    )(x2d, norm_w, w_a, w_bz, w_bg)


def _t5_bucket(dist):
    max_exact = N_BUCKETS // 2
    d = np.maximum(dist, 1).astype(np.float32)
    large = max_exact + (np.log(d / max_exact) / np.log(REL_MAX_DIST / max_exact)
                         * (N_BUCKETS - max_exact)).astype(np.int32)
    large = np.minimum(large, N_BUCKETS - 1)
    return np.where(dist < max_exact, dist, large).astype(np.int32)


def _group_bias(rel_bias, g):
    dil = GROUPS[g][1]
    bucket = _t5_bucket(np.arange(N_KEYS) * dil)
    return rel_bias[bucket][:, g * H_G:(g + 1) * H_G].astype(F32)


def _prompt_bias(rel_bias, g):
    period = 3 * Q_BLOCK
    vals = _group_bias(rel_bias, g)
    u = jnp.concatenate([vals[::-1], jnp.full((period - N_KEYS, H_G), NEG, F32)], axis=0).T
    flat = jnp.tile(u, (1, Q_BLOCK))[:, :Q_BLOCK * (period - 1)]
    return flat.reshape(H_G, Q_BLOCK, period - 1)[:, :, :2 * Q_BLOCK]


def _attn_group(g, a_ref, bias_ref, emit):
    dil, lc, _ = a_ref.shape
    nb = lc // Q_BLOCK
    lane_head = lax.broadcasted_iota(jnp.int32, (1, A_OUT), 1) // HD_A
    hms = [lane_head == h for h in range(H_G)]
    blocks = [(r, pb) for r in range(dil) for pb in range(nb)]
    for i in range(0, len(blocks), 2):
        items = blocks[i:i + 2]
        work = [(it, h) for it in items for h in range(H_G)]
        keys = lambda r, pb, part: a_ref[r, pl.ds((pb - 1) * Q_BLOCK, 2 * Q_BLOCK) if pb > 0
                                         else pl.ds(0, Q_BLOCK), part * A_OUT:(part + 1) * A_OUT]
        s = {}
        for (r, pb), h in work:
            q = a_ref[r, pl.ds(pb * Q_BLOCK, Q_BLOCK), 0:A_OUT]
            qh = jnp.where(hms[h], q, jnp.zeros_like(q))
            b = bias_ref[h] if pb > 0 else bias_ref[h, :, Q_BLOCK:2 * Q_BLOCK]
            s[r, pb, h] = _dot_nt(qh, keys(r, pb, 1)) + b
        pn, lse = {}, {}
        for (r, pb), h in work:
            m = jnp.max(s[r, pb, h], axis=-1, keepdims=True)
            p = jnp.exp(s[r, pb, h] - m)
            den = jnp.sum(p, axis=-1, keepdims=True)
            pn[r, pb, h] = (p / den).astype(BF16)
            lse[r, pb, h] = m + jnp.log(den)
        oh = {}
        for (r, pb), h in work:
            oh[r, pb, h] = jnp.dot(pn[r, pb, h], keys(r, pb, 2), preferred_element_type=F32)
        for r, pb in items:
            o_acc, l_acc = oh[r, pb, 0], lse[r, pb, 0]
            for h in range(1, H_G):
                o_acc = jnp.where(hms[h], oh[r, pb, h], o_acc)
                l_acc = jnp.where(hms[h], lse[r, pb, h], l_acc)
            emit(r, pb, o_acc, jnp.broadcast_to(l_acc, (Q_BLOCK, A_OUT)))


def _attn_prompt_kernel(a0_ref, a1_ref, a2_ref, b0_ref, b1_ref, b2_ref, o_ref,
                        od_ref, ld_ref, orun_ref, lrun_ref):
    L = o_ref.shape[0]
    halves = A_OUT // LANES
    for g, (a_ref, bias_ref) in enumerate(((a0_ref, b0_ref), (a1_ref, b1_ref), (a2_ref, b2_ref))):
        dil = a_ref.shape[0]
        lc = L // dil
        if dil == 1:
            assert g == 0

            def emit(r, pb, o, lse):
                for c in range(halves):
                    orun_ref[c, pl.ds(pb * Q_BLOCK, Q_BLOCK), :] = o[:, c * LANES:(c + 1) * LANES]
                    lrun_ref[c, pl.ds(pb * Q_BLOCK, Q_BLOCK), :] = lse[:, c * LANES:(c + 1) * LANES]

            _attn_group(g, a_ref, bias_ref, emit)
            continue

        def emit(r, pb, o, lse, lc=lc):
            od_ref[pl.ds(r * lc + pb * Q_BLOCK, Q_BLOCK), :] = o
            ld_ref[pl.ds(r * lc + pb * Q_BLOCK, Q_BLOCK), :] = lse

        _attn_group(g, a_ref, bias_ref, emit)
        for r in range(dil):
            rows = pl.ds(r, lc, stride=dil)
            src = pl.ds(r * lc, lc)
            for c in range(halves):
                cols = pl.ds(c * LANES, LANES)
                o_new, l_new = od_ref[src, cols], ld_ref[src, cols]
                o_old, l_old = orun_ref[c, rows, :], lrun_ref[c, rows, :]
                m = jnp.maximum(l_old, l_new)
                e_old, e_new = jnp.exp(l_old - m), jnp.exp(l_new - m)
                den = e_old + e_new
                orun_ref[c, rows, :] = (e_old * o_old + e_new * o_new) / den
                if g < N_GROUPS - 1:
                    lrun_ref[c, rows, :] = m + jnp.log(den)
    for c in range(halves):
        o_ref[:, c * LANES:(c + 1) * LANES] = orun_ref[c].astype(o_ref.dtype)


def _attn_prompt(a_groups, biases, B, L):
    halves = A_OUT // LANES
    a_specs = [pl.BlockSpec((None,) + a.shape[1:], lambda b: (b, 0, 0, 0)) for a in a_groups]
    bias_spec = pl.BlockSpec((H_G, Q_BLOCK, 2 * Q_BLOCK), lambda b: (0, 0, 0))
    return pl.pallas_call(
        _attn_prompt_kernel,
        grid=(B,),
        in_specs=a_specs + [bias_spec] * N_GROUPS,
        out_specs=pl.BlockSpec((L, A_OUT), lambda b: (b, 0)),
        out_shape=jax.ShapeDtypeStruct((B * L, A_OUT), BF16),
        scratch_shapes=[pltpu.VMEM((L, A_OUT), F32), pltpu.VMEM((L, A_OUT), F32),
                        pltpu.VMEM((halves, L, LANES), F32), pltpu.VMEM((halves, L, LANES), F32)],
        compiler_params=_cparams(("parallel",)),
        name="attn_prompt",
    )(*a_groups, *biases)


def _delta_prompt_kernel(bq_ref, bk_ref, bv_ref, z_ref, bg_ref, cw_ref, hp_ref, nd_ref,
                         ob_ref, sout_ref, s_ref, xc_ref):
    c = pl.program_id(1)
    C = CHUNK
    n_seq = bq_ref.shape[0]

    @pl.when(c == 0)
    def _():
        s_ref[...] = jnp.zeros_like(s_ref)
        xc_ref[:, 0:8, :] = jnp.zeros((n_seq, 8, CONV_DIM), F32)

    xc_ref[:, 8:8 + C, 0:B_QK] = bq_ref[...]
    xc_ref[:, 8:8 + C, B_QK:2 * B_QK] = bk_ref[...]
    xc_ref[:, 8:8 + C, 2 * B_QK:CONV_DIM] = bv_ref[...]

    def conv(i, lo):
        y = cw_ref[0:1, lo:lo + DK] * xc_ref[i, 5:5 + C, lo:lo + DK]
        for j in range(1, CONV_W):
            y = y + cw_ref[j:j + 1, lo:lo + DK] * xc_ref[i, 5 + j:5 + j + C, lo:lo + DK]
        return _silu(y)

    hp = hp_ref[...]
    row = lax.broadcasted_iota(jnp.int32, (C, C), 0)
    colc = lax.broadcasted_iota(jnp.int32, (C, C), 1)
    tril = row >= colc
    tril_f = tril.astype(F32)
    eye = (row == colc).astype(F32)
    r2 = lax.broadcasted_iota(jnp.int32, (LANES, LANES), 0)
    c2 = lax.broadcasted_iota(jnp.int32, (LANES, LANES), 1)
    eye_l = (r2 == c2).astype(F32)
    nd = nd_ref[...]

    units = [(i, h) for i in range(n_seq) for h in range(HB)]
    idx = range(len(units))
    ks, kqs, vbs, kbgs, qgs, kdts, decays, elasts = [], [], [], [], [], [], [], []
    for i in range(n_seq):
        bg = bg_ref[i]
        beta_all = _sigmoid(bg)
        g_all = -jnp.exp(hp[0:1, :]) * _softplus(bg + hp[1:2, :])
        gc = _dot_f32(tril_f, g_all)
        gct = lax.dot_general(eye_l, gc, (((1,), (1,)), ((), ())), preferred_element_type=F32,
                              precision=lax.Precision.HIGHEST)
        for h in range(HB):
            lo = h * DK
            q = conv(i, lo)
            k = conv(i, B_QK + lo)
            v = conv(i, 2 * B_QK + lo)
            q = q * (lax.rsqrt(jnp.sum(q * q, axis=-1, keepdims=True) + EPS) * (DK ** -0.5))
            k = k * lax.rsqrt(jnp.sum(k * k, axis=-1, keepdims=True) + EPS)
            beta = beta_all[:, h:h + 1]
            gcol = gc[:, HB + h:HB + h + 1]
            grow = gct[HB + h:HB + h + 1, :]
            glast = gc[C - 1:C, HB + h:HB + h + 1]
            egc = jnp.exp(gcol)
            kb = k * beta
            ks.append(k.astype(BF16))
            kqs.append(jnp.concatenate([kb, q], axis=0).astype(BF16))
            vbs.append(v * beta)
            kbgs.append(kb * egc)
            qgs.append(q * egc)
            kdts.append((k * jnp.exp(glast - gcol)).T)
            decays.append(jnp.exp(jnp.where(tril, gcol - grow, NEG)))
            elasts.append(jnp.exp(glast))
    kq = [_dot_nt(kqs[u], ks[u]) for u in idx]
    pw = [jnp.where(row > colc, kq[u][0:C] * decays[u], 0.0) for u in idx]
    attn = [kq[u][C:2 * C] * decays[u] for u in idx]
    p = [eye - pw[u] for u in idx]
    pw = [_dot(pw[u], pw[u]) for u in idx]
    n = 4
    while n < C:
        nxt = [_dot(pw[u], pw[u]) for u in idx]
        p = [p[u] + _dot(p[u], pw[u]) for u in idx]
        pw = nxt
        n *= 2
    p = [p[u] + _dot(p[u], pw[u]) for u in idx]
    uw = [_dot(p[u], jnp.concatenate([vbs[u], kbgs[u]], axis=1)) for u in idx]
    s_old = [s_ref[i, h] for i, h in units]
    ws_qs = [_dot(jnp.concatenate([uw[u][:, DV:DV + DK], qgs[u]], axis=0), s_old[u]) for u in idx]
    v_new = [uw[u][:, 0:DV] - ws_qs[u][0:C] for u in idx]
    av = [_dot(jnp.concatenate([attn[u], kdts[u]], axis=0), v_new[u]) for u in idx]
    for u, (i, h) in enumerate(units):
        lo = h * DK
        s_ref[i, h] = s_old[u] * elasts[u] + av[u][C:C + DK]
        o = ws_qs[u][C:2 * C] + av[u][0:C]
        o = o * lax.rsqrt(jnp.mean(o * o, axis=-1, keepdims=True) + EPS) * nd
        ob_ref[i, :, lo:lo + DV] = (o * _silu(z_ref[i, :, lo:lo + DV].astype(F32))).astype(ob_ref.dtype)

    xc_ref[:, 0:8, :] = xc_ref[:, C:C + 8, :]

    @pl.when(c == pl.num_programs(1) - 1)
    def _():
        sout_ref[...] = s_ref[...]


DELTA_SEQS = 8


def _delta_prompt(b_qkv, zg, bg, conv_w, head_params, norm_delta, l, B, L):
    n = L // CHUNK
    ns = DELTA_SEQS if B % DELTA_SEQS == 0 else 1
    seq = lambda t: t.reshape(t.shape[:-2] + (B, L, t.shape[-1]))
    rows = lambda w, part: pl.BlockSpec((ns, CHUNK, w), lambda b, c: (b, c, part))
    slab = lambda part: pl.BlockSpec((None, ns, CHUNK, PROJ_TW), lambda b, c: (part, b, c, 0))
    o_b, s_fin = pl.pallas_call(
        _delta_prompt_kernel,
        grid=(B // ns, n),
        in_specs=[
            slab(0), slab(1), slab(2),
            slab(0),
            rows(LANES, 0),
            pl.BlockSpec((None, CONV_W, CONV_DIM), lambda b, c: (l, 0, 0)),
            pl.BlockSpec((None, 2, LANES), lambda b, c: (l, 0, 0)),
            pl.BlockSpec((None, 1, DV), lambda b, c: (l, 0, 0)),
        ],
        out_specs=[
            rows(B_V, 0),
            pl.BlockSpec((ns, HB, DK, DV), lambda b, c: (b, 0, 0, 0)),
        ],
        out_shape=[
            jax.ShapeDtypeStruct((B, L, B_V), BF16),
            jax.ShapeDtypeStruct((B, HB, DK, DV), F32),
        ],
        scratch_shapes=[pltpu.VMEM((ns, HB, DK, DV), F32), pltpu.VMEM((ns, 8 + CHUNK, CONV_DIM), F32)],
        compiler_params=_cparams(("parallel", "arbitrary")),
        name="delta_prompt",
    )(seq(b_qkv), seq(b_qkv), seq(b_qkv), seq(zg), seq(bg), conv_w, head_params, norm_delta)
    return o_b.reshape(B * L, B_V), s_fin


def _merge_groups(os_, lses):
    m = jnp.maximum(jnp.maximum(lses[0], lses[1]), lses[2])
    es = [jnp.exp(t - m) for t in lses]
    den = es[0] + es[1] + es[2]
    return (es[0] * os_[0] + es[1] * os_[1] + es[2] * os_[2]) / den


def _mix_kernel(oa_ref, ob_ref, ga_ref, gb_ref, x_ref, wa_ref, wb_ref, wo_ref, nw_ref, out_ref):
    ya = _dot(oa_ref[...], wa_ref[...])
    yb = jnp.dot(ob_ref[...], wb_ref[...], preferred_element_type=F32)
    mix = _sigmoid(ga_ref[...].astype(F32)) * ya + _sigmoid(gb_ref[...].astype(F32)) * yb
    y = _dot(mix, wo_ref[...])
    out_ref[...] = x_ref[...] + _rms(y, nw_ref[...])


def _mix(o_a, o_b, zg, x2d, wa, wb, wo, norm_w, l, tm):
    T = x2d.shape[0]
    rows = lambda w: pl.BlockSpec((tm, w), lambda i: (i, 0))
    return pl.pallas_call(
        _mix_kernel,
        grid=(T // tm,),
        in_specs=[
            rows(A_OUT),
            rows(B_V),
            pl.BlockSpec((None, tm, D_MODEL), lambda i: (1, i, 0)),
            pl.BlockSpec((None, tm, D_MODEL), lambda i: (2, i, 0)),
            rows(D_MODEL),
            pl.BlockSpec((None, A_OUT, D_MODEL), lambda i: (l, 0, 0)),
            pl.BlockSpec((None, B_V, D_MODEL), lambda i: (l, 0, 0)),
            pl.BlockSpec((None, D_MODEL, D_MODEL), lambda i: (l, 0, 0)),
            pl.BlockSpec((None, 1, D_MODEL), lambda i: (l, 0, 0)),
        ],
        out_specs=rows(D_MODEL),
        out_shape=jax.ShapeDtypeStruct((T, D_MODEL), F32),
        compiler_params=_cparams(("parallel",)),
        name="mix",
    )(o_a, o_b, zg, zg, x2d, wa, wb, wo, norm_w)


def _ffn_kernel(x_ref, nw1_ref, wg_ref, wu_ref, wd_ref, nw2_ref, out_ref, h_ref, acc_ref):
    j = pl.program_id(1)

    @pl.when(j == 0)
    def _():
        h_ref[...] = _rms(x_ref[...], nw1_ref[...]).astype(BF16)
        acc_ref[...] = jnp.zeros_like(acc_ref)

    h = h_ref[...]
    gt = jnp.dot(h, wg_ref[...], preferred_element_type=F32)
    up = jnp.dot(h, wu_ref[...], preferred_element_type=F32)
    acc_ref[...] += _dot(_silu(gt) * up, wd_ref[...])

    @pl.when(j == pl.num_programs(1) - 1)
    def _():
        out_ref[...] = x_ref[...] + _rms(acc_ref[...], nw2_ref[...])


def _ffn(x2d, nw1, w_in, w_out, nw2, l, tm):
    T = x2d.shape[0]
    nf = D_FF // FFN_TF
    return pl.pallas_call(
        _ffn_kernel,
        grid=(T // tm, nf),
        in_specs=[
            pl.BlockSpec((tm, D_MODEL), lambda i, j: (i, 0)),
            pl.BlockSpec((None, 1, D_MODEL), lambda i, j: (l, 0, 0)),
            pl.BlockSpec((None, D_MODEL, FFN_TF), lambda i, j: (l, 0, j)),
            pl.BlockSpec((None, D_MODEL, FFN_TF), lambda i, j: (l, 0, nf + j)),
            pl.BlockSpec((None, FFN_TF, D_MODEL), lambda i, j: (l, j, 0)),
            pl.BlockSpec((None, 1, D_MODEL), lambda i, j: (l, 0, 0)),
        ],
        out_specs=pl.BlockSpec((tm, D_MODEL), lambda i, j: (i, 0)),
        out_shape=jax.ShapeDtypeStruct((T, D_MODEL), F32),
        scratch_shapes=[pltpu.VMEM((tm, D_MODEL), BF16), pltpu.VMEM((tm, D_MODEL), F32)],
        compiler_params=_cparams(("parallel", "arbitrary")),
        name="ffn",
    )(x2d, nw1, w_in, w_in, w_out, nw2)


N_CACHED = N_KEYS - 1


def _attn_sample_kernel(a_ref, c0_ref, c1_ref, c2_ref, b0_ref, b1_ref, b2_ref, bnew_ref, seg_ref, o_ref):
    caches = (c0_ref, c1_ref, c2_ref)
    biases = (b0_ref, b1_ref, b2_ref)
    seg = seg_ref[...]
    a = a_ref[0]
    gh = [(g, h) for g in range(N_GROUPS) for h in range(H_G)]
    rows8 = lambda t: jnp.broadcast_to(t, (8, t.shape[-1]))
    q, vn, s_new = [], [], []
    for g in range(N_GROUPS):
        lo = g * A_GRP
        q.append(a[:, lo:lo + A_OUT] * ATTN_SCALE)
        vn.append(a[:, lo + 2 * A_OUT:lo + 3 * A_OUT])
        s_new.append(_dot(rows8(q[g] * a[:, lo + A_OUT:lo + 2 * A_OUT]), seg) + bnew_ref[g])
    head = lambda t, h: t[:, h * HD_A:(h + 1) * HD_A]
    s = {(g, h): _dot(rows8(head(q[g], h)), caches[g][0, h]) + biases[g][h:h + 1, :] for g, h in gh}
    pn, p_new, lse = {}, {}, {}
    for g, h in gh:
        sn = s_new[g][:, h * HD_A:h * HD_A + 1]
        m = jnp.maximum(jnp.max(s[g, h], axis=-1, keepdims=True), sn)
        p = jnp.exp(s[g, h] - m)
        e_new = jnp.exp(sn - m)
        den = jnp.sum(p, axis=-1, keepdims=True) + e_new
        pn[g, h] = p / den
        p_new[g, h] = e_new / den
        lse[g, h] = m + jnp.log(den)
    o = {(g, h): _dot_nt(pn[g, h], caches[g][1, h]) + p_new[g, h] * head(vn[g], h) for g, h in gh}
    merged = [_merge_groups([o[g, h] for g in range(N_GROUPS)], [lse[g, h] for g in range(N_GROUPS)])
              for h in range(H_G)]
    o_ref[0] = jnp.concatenate(merged, axis=1)[0:1, :]


def _attn_sample(a_qkv, caches_t, biases, bias_new, seg, l):
    Bd = a_qkv.shape[0]
    cache_specs = [pl.BlockSpec((None, None) + c.shape[2:], lambda i: (l, i, 0, 0, 0, 0)) for c in caches_t]
    bias_specs = [pl.BlockSpec(b.shape, lambda i: (0, 0)) for b in biases]
    o = pl.pallas_call(
        _attn_sample_kernel,
        grid=(Bd,),
        in_specs=[pl.BlockSpec((1, 1, A_QKV), lambda i: (i, 0, 0))] + cache_specs + bias_specs + [
            pl.BlockSpec((N_GROUPS, 1, A_OUT), lambda i: (0, 0, 0)),
            pl.BlockSpec((A_OUT, A_OUT), lambda i: (0, 0)),
        ],
        out_specs=pl.BlockSpec((1, 1, A_OUT), lambda i: (i, 0, 0)),
        out_shape=jax.ShapeDtypeStruct((Bd, 1, A_OUT), F32),
        compiler_params=_cparams(("parallel",)),
        name="attn_sample",
    )(a_qkv.reshape(Bd, 1, A_QKV), *caches_t, *biases, bias_new, seg)
    return o.reshape(Bd, A_OUT)


def _sample_bias(rel_bias):
    rows, new = [], []
    for g, (win, dil) in enumerate(GROUPS):
        vals = _group_bias(rel_bias, g)
        sel = vals[1:][::-1][:, None, :]
        skip = jnp.full((N_CACHED, dil - 1, H_G), NEG, F32)
        rows.append(jnp.concatenate([sel, skip], axis=1).reshape(N_CACHED * dil, H_G).T)
        new.append(jnp.repeat(vals[0:1], HD_A, axis=1))
    return rows, jnp.stack(new)


def _delta_sample_kernel(bn_ref, cb_ref, z_ref, bg_ref, s_ref, cw_ref, hp_ref, nd_ref, ob_ref, sout_ref):
    bb = bn_ref.shape[0]
    bg = bg_ref[...]
    hp = hp_ref[...]
    beta_all = _sigmoid(bg)
    g_all = -jnp.exp(hp[0:1, :]) * _softplus(bg + hp[1:2, :])
    eg_all = jnp.exp(g_all)
    nd = nd_ref[...]
    r8 = lax.broadcasted_iota(jnp.int32, (1, 8, 1), 1)
    r2 = lax.broadcasted_iota(jnp.int32, (DK, DK), 0)
    c2 = lax.broadcasted_iota(jnp.int32, (DK, DK), 1)
    eye = jnp.broadcast_to((r2 == c2).astype(BF16)[None], (bb, DK, DK))

    def conv(lo):
        y = cw_ref[CONV_W - 1:CONV_W, lo:lo + DK] * bn_ref[:, :, lo:lo + DK]
        for j in range(CONV_W - 1):
            y = y + cw_ref[j:j + 1, lo:lo + DK] * cb_ref[:, j:j + 1, lo:lo + DK]
        return _silu(y)

    for h in range(HB):
        lo = h * DK
        q = conv(lo)
        k = conv(B_QK + lo)
        v = conv(2 * B_QK + lo)
        q = q * lax.rsqrt(jnp.sum(q * q, axis=-1, keepdims=True) + EPS) * (DK ** -0.5)
        k = k * lax.rsqrt(jnp.sum(k * k, axis=-1, keepdims=True) + EPS)
        beta = beta_all[:, :, h:h + 1]
        eg = eg_all[:, :, HB + h:HB + h + 1]
        s = s_ref[:, h]
        w = k * (beta * eg)
        qg = q * eg
        lhs = jnp.where(r8 == 0, w, jnp.where(r8 == 1, qg, 0.0))
        r = jnp.einsum('bmk,bkv->bmv', lhs.astype(BF16), s.astype(BF16), preferred_element_type=F32)
        v_new = v * beta - r[:, 0:1, :]
        attn = jnp.sum(q * k, axis=-1, keepdims=True)
        o = r[:, 1:2, :] + attn * v_new
        kt = jnp.einsum('bij,bmj->bim', eye, jnp.broadcast_to(k, (bb, 8, DK)).astype(BF16),
                        preferred_element_type=F32)
        sout_ref[:, h] = s * eg + kt[:, :, 0:1] * v_new
        o = o * lax.rsqrt(jnp.mean(o * o, axis=-1, keepdims=True) + EPS) * nd
        ob_ref[:, :, lo:lo + DV] = (o * _silu(z_ref[:, :, lo:lo + DV].astype(F32))).astype(ob_ref.dtype)


def _delta_sample(b_qkv, conv_state, zg, bg, state, conv_w, head_params, norm_delta, l, bb):
    Bd = b_qkv.shape[0]
    row = lambda w: pl.BlockSpec((bb, 1, w), lambda i: (i, 0, 0))
    o_b, s_new = pl.pallas_call(
        _delta_sample_kernel,
        grid=(Bd // bb,),
        in_specs=[
            row(CONV_DIM),
            pl.BlockSpec((None, bb, CONV_W - 1, CONV_DIM), lambda i: (l, i, 0, 0)),
            row(B_V),
            row(LANES),
            pl.BlockSpec((None, bb, HB, DK, DV), lambda i: (l, i, 0, 0, 0)),
            pl.BlockSpec((None, CONV_W, CONV_DIM), lambda i: (l, 0, 0)),
            pl.BlockSpec((None, 2, LANES), lambda i: (l, 0, 0)),
            pl.BlockSpec((None, 1, DV), lambda i: (l, 0, 0)),
        ],
        out_specs=[
            row(B_V),
            pl.BlockSpec((bb, HB, DK, DV), lambda i: (i, 0, 0, 0)),
        ],
        out_shape=[
            jax.ShapeDtypeStruct((Bd, 1, B_V), BF16),
            jax.ShapeDtypeStruct((Bd, HB, DK, DV), F32),
        ],
        compiler_params=_cparams(("parallel",)),
        name="delta_sample",
    )(b_qkv.reshape(Bd, 1, CONV_DIM), conv_state, zg[0].reshape(Bd, 1, B_V), bg.reshape(Bd, 1, LANES),
      state, conv_w, head_params, norm_delta)
    return o_b.reshape(Bd, B_V), s_new


def _prepare(p):
    w_in = p['w_in']
    tiles = lambda t, width: jnp.transpose(
        t.astype(BF16).reshape(DEPTH, D_MODEL, t.shape[-1] // width, width), (0, 2, 1, 3))
    w_a = tiles(w_in[:, :, :COL_B], PROJ_TN)
    w_bz = tiles(jnp.concatenate([w_in[:, :, COL_B:COL_BG], w_in[:, :, COL_GATE:]], axis=-1), PROJ_TW)
    w_bg = jnp.pad(w_in[:, :, COL_BG:COL_GATE], ((0, 0), (0, 0), (0, LANES - 2 * HB))).astype(BF16)
    pad = lambda t: jnp.pad(t.astype(F32), ((0, 0), (HB, LANES - 2 * HB)))
    head_params = jnp.stack([pad(p['a_log']), pad(p['dt_bias'])], axis=1)
    vec = lambda t: t.astype(F32)[:, None, :]
    return dict(
        w_a=w_a, w_bz=w_bz, w_bg=w_bg, head_params=head_params,
        conv_w=p['conv_w'].astype(F32),
        wa=p['w_branch_a'].astype(BF16), wb=p['w_branch_b'].astype(BF16), wo=p['w_out'].astype(BF16),
        w_ffn_in=p['w_ffn_in'].astype(BF16), w_ffn_out=p['w_ffn_out'].astype(BF16),
        norm_pre_mix=vec(p['norm_pre_mix']), norm_post_mix=vec(p['norm_post_mix']),
        norm_pre_ffn=vec(p['norm_pre_ffn']), norm_post_ffn=vec(p['norm_post_ffn']),
        norm_delta=vec(p['norm_delta']),
    )


def _kv_rows(a_qkv, B, L, g, rows):
    t = a_qkv.reshape(B, L, N_GROUPS, 3, H_G, HD_A)
    return t[:, L - rows:, g, 1:]


def _trunk_prompt(x, w, rel_bias):
    B, L, _ = x.shape
    T = B * L
    tm = min(1024, T)
    x2d = x.reshape(T, D_MODEL)
    biases = [_prompt_bias(rel_bias, g) for g in range(N_GROUPS)]
    kv_out = [[] for _ in range(N_GROUPS)]
    s_out, conv_out = [], []
    for l in range(DEPTH):
        a0, a1, a2, b_qkv, zg, bg, kvt = _in_proj(x2d, w['norm_pre_mix'], w['w_a'], w['w_bz'], w['w_bg'], l, tm,
                                                  seq_len=L)
        o_a = _attn_prompt([a0, a1, a2], biases, B, L)
        o_b, s_fin = _delta_prompt(b_qkv, zg, bg, w['conv_w'], w['head_params'], w['norm_delta'], l, B, L)
        x2d = _mix(o_a, o_b, zg, x2d, w['wa'], w['wb'], w['wo'], w['norm_post_mix'], l, min(512, T))
        x2d = _ffn(x2d, w['norm_pre_ffn'], w['w_ffn_in'], w['w_ffn_out'], w['norm_post_ffn'], l, min(512, T))
        for g, (win, _) in enumerate(GROUPS):
            rows = min(win, L)
            kv_out[g].append(kvt[:, g, :, L - rows:].reshape(B, 2, H_G, HD_A, rows))
        s_out.append(s_fin)
        tail = b_qkv.reshape(N_TILES_B, B, L, PROJ_TW)[:, :, L - (CONV_W - 1):]
        conv_out.append(jnp.transpose(tail, (1, 2, 0, 3)).reshape(B, CONV_W - 1, CONV_DIM))
    kv = [jnp.transpose(jnp.stack(t), (0, 1, 5, 2, 3, 4)) for t in kv_out]
    return (x2d.reshape(B, L, D_MODEL), kv[0], kv[1], kv[2], jnp.stack(s_out), jnp.stack(conv_out))


def _trunk_sample(x, caches, state, conv_state, w, rel_bias):
    Bd = x.shape[0]
    x2d = x.reshape(Bd, D_MODEL)
    biases, bias_new = _sample_bias(rel_bias)
    lane_head = np.arange(A_OUT) // HD_A
    seg = jnp.asarray(lane_head[:, None] == lane_head[None, :], BF16)
    bb = min(8, Bd)
    for c, (win, dil) in zip(caches, GROUPS):
        assert c.shape[2] == N_CACHED * dil, "cache length must equal the group's window"
    caches_t = [jnp.transpose(c, (0, 1, 3, 4, 5, 2)) for c in caches]
    kv_out = [[] for _ in range(N_GROUPS)]
    s_out, conv_out = [], []
    for l in range(DEPTH):
        a_qkv, b_slabs, zg, bg = _in_proj(x2d, w['norm_pre_mix'], w['w_a'], w['w_bz'], w['w_bg'], l, Bd)
        b_qkv = jnp.transpose(b_slabs, (1, 0, 2)).reshape(Bd, CONV_DIM)
        o_a = _attn_sample(a_qkv, caches_t, biases, bias_new, seg, l)
        o_b, s_new = _delta_sample(b_qkv, conv_state, zg, bg, state, w['conv_w'], w['head_params'],
                                   w['norm_delta'], l, bb)
        x2d = _mix(o_a, o_b, zg, x2d, w['wa'], w['wb'], w['wo'], w['norm_post_mix'], l, Bd)
        x2d = _ffn(x2d, w['norm_pre_ffn'], w['w_ffn_in'], w['w_ffn_out'], w['norm_post_ffn'], l, Bd)
        for g in range(N_GROUPS):
            kv_out[g].append(_kv_rows(a_qkv, Bd, 1, g, 1))
        s_out.append(s_new)
        conv_out.append(jnp.concatenate([conv_state[l][:, 1:], b_qkv[:, None, :]], axis=1))
    return (x2d.reshape(Bd, 1, D_MODEL), jnp.stack(kv_out[0]), jnp.stack(kv_out[1]), jnp.stack(kv_out[2]),
            jnp.stack(s_out), jnp.stack(conv_out))


def kernel(x_prompt, x_sample, cache_kv_w128, cache_kv_w512, cache_kv_w2048, state_delta, state_conv,
           rel_bias, norm_pre_mix, w_in, conv_w, a_log, dt_bias, norm_delta, w_branch_a, w_branch_b,
           w_out, norm_post_mix, norm_pre_ffn, w_ffn_in, w_ffn_out, norm_post_ffn):
    p = dict(w_in=w_in, conv_w=conv_w, a_log=a_log, dt_bias=dt_bias, norm_delta=norm_delta,
             w_branch_a=w_branch_a, w_branch_b=w_branch_b, w_out=w_out, norm_pre_mix=norm_pre_mix,
             norm_post_mix=norm_post_mix, norm_pre_ffn=norm_pre_ffn, w_ffn_in=w_ffn_in,
             w_ffn_out=w_ffn_out, norm_post_ffn=norm_post_ffn)
    w = _prepare(p)
    assert x_sample.shape[1] == 1, "the decode trunk handles one new token per sequence"
    y_p, kv0_p, kv1_p, kv2_p, s_p, conv_p = _trunk_prompt(x_prompt, w, rel_bias)
    y_s, kv0_s, kv1_s, kv2_s, s_s, conv_s = _trunk_sample(
        x_sample, (cache_kv_w128, cache_kv_w512, cache_kv_w2048), state_delta, state_conv, w, rel_bias)
    return (y_p, y_s, kv0_p, kv1_p, kv2_p, s_p, conv_p, kv0_s, kv1_s, kv2_s, s_s, conv_s)
```
